```python
import math
import jax, jax.numpy as jnp
from jax import lax
import numpy as np

D_MODEL = 1024
BATCH = 16
SEQ = 4096
DEPTH = 4

GRID_W = 64
CTX_LEN = 256
D_MIX = D_MODEL
NA_HEADS = 8
NA_HEAD_DIM = 64
NA_WIDTH = NA_HEADS * NA_HEAD_DIM
NA_KR = 8
NA_KC = 16
DN_HEAD_DIM = 128
DN_WIDTH = D_MIX - NA_WIDTH
DN_HEADS = DN_WIDTH // DN_HEAD_DIM
DN_CONV = 5
DN_CHUNK = 64
ROPE_THETA = 10000.0
D_FF_DENSE = 2816
N_EXPERTS = 8
TOP_K = 2
D_FF_EXPERT = 3584
N_DENSE = (DEPTH + 1) // 2
N_MOE = DEPTH // 2
DEEPNORM_ALPHA = (2.0 * DEPTH) ** 0.25
DEEPNORM_BETA = (8.0 * DEPTH) ** -0.25
P_IN = 3 * NA_WIDTH + 4 * DN_WIDTH + 4 * DN_HEADS
LN_EPS = 1e-5
RMS_EPS = 1e-6

kernel_name = 'hybrid_natten_gdn_moe_diffusion_block'


def _layernorm(x, g, b):
    xf = x.astype(jnp.float32)
    mu = jnp.mean(xf, -1, keepdims=True)
    var = jnp.mean(jnp.square(xf - mu), -1, keepdims=True)
    return ((xf - mu) * lax.rsqrt(var + LN_EPS)).astype(x.dtype) * g + b


def _rmsnorm(x, g):
    xf = x.astype(jnp.float32)
    return (xf * lax.rsqrt(jnp.mean(xf * xf, -1, keepdims=True) + RMS_EPS)).astype(x.dtype) * g


def _l2norm(x):
    return x * lax.rsqrt(jnp.sum(x * x, -1, keepdims=True) + RMS_EPS)


def _modulate(x, shift, scale):
    return x * (1.0 + scale) + shift


def _axial_rope(x, rows, cols):
    half = x.shape[-1] // 2
    nf = half // 2
    inv = ROPE_THETA ** (-jnp.arange(nf, dtype=jnp.float32) / nf)

    def rot(xa, pos):
        ang = pos.astype(jnp.float32)[:, None] * inv
        cos = jnp.cos(ang)[None, :, None, :]
        sin = jnp.sin(ang)[None, :, None, :]
        x1, x2 = xa[..., :nf], xa[..., nf:]
        return jnp.concatenate([x1 * cos - x2 * sin, x1 * sin + x2 * cos], -1)

    return jnp.concatenate([rot(x[..., :half], rows), rot(x[..., half:], cols)], -1)


def _short_conv(x, w):
    return lax.conv_general_dilated(x, w[:, None, :].astype(x.dtype), window_strides=(1,), padding='SAME',
                                    dimension_numbers=('NWC', 'WIO', 'NWC'), feature_group_count=x.shape[-1])


def _neighbourhood_attention(q, k, v, kc, vc, rpb):
    B, N, H, d = q.shape
    rows = N // GRID_W
    kr = min(NA_KR, rows)
    n_cb = GRID_W // NA_KC
    kw = 2 * NA_KC
    nk = kr * kw
    scale = d ** -0.5
    qg = q.reshape(B, rows, n_cb, NA_KC, H, d)
    kg = k.reshape(B, rows, GRID_W, H, d)
    vg = v.reshape(B, rows, GRID_W, H, d)
    qcol = jnp.arange(GRID_W).reshape(n_cb, NA_KC)
    kstart = jnp.clip(jnp.arange(n_cb) * NA_KC - NA_KC // 2, 0, GRID_W - kw)
    kcol = kstart[:, None] + jnp.arange(kw)
    cstart = jnp.clip(qcol - NA_KC // 2, 0, GRID_W - NA_KC)
    col_ok = (kcol[:, None, :] >= cstart[..., None]) & (kcol[:, None, :] < cstart[..., None] + NA_KC)
    col_ok = jnp.broadcast_to(col_ok[:, :, None, :], (n_cb, NA_KC, kr, kw)).reshape(n_cb, NA_KC, nk)
    dc_idx = jnp.clip(kcol[:, None, :] - qcol[..., None] + NA_KC - 1, 0, 2 * NA_KC - 2)

    def row_block(r):
        rs = jnp.clip(r - kr // 2, 0, rows - kr)
        k_blk = lax.dynamic_slice_in_dim(kg, rs, kr, axis=1)[:, :, kcol]
        v_blk = lax.dynamic_slice_in_dim(vg, rs, kr, axis=1)[:, :, kcol]
        k_blk = jnp.swapaxes(k_blk, 1, 2).reshape(B, n_cb, nk, H, d)
        v_blk = jnp.swapaxes(v_blk, 1, 2).reshape(B, n_cb, nk, H, d)
        q_blk = lax.dynamic_index_in_dim(qg, r, axis=1, keepdims=False)
        dr_idx = rs + jnp.arange(kr) - r + NA_KR - 1
        bias = rpb[:, dr_idx[None, None, :, None], dc_idx[:, :, None, :]].reshape(H, n_cb, NA_KC, nk)
        s_loc = jnp.einsum('bjqhd,bjkhd->bhjqk', q_blk, k_blk).astype(jnp.float32) * scale + bias
        s_loc = jnp.where(col_ok, s_loc, -jnp.inf)
        s_ctx = jnp.einsum('bjqhd,bchd->bhjqc', q_blk, kc).astype(jnp.float32) * scale
        p = jax.nn.softmax(jnp.concatenate([s_loc, s_ctx], -1), axis=-1).astype(v.dtype)
        o = (jnp.einsum('bhjqk,bjkhd->bjqhd', p[..., :nk], v_blk)
             + jnp.einsum('bhjqc,bchd->bjqhd', p[..., nk:], vc))
        return o.reshape(B, GRID_W, H, d)

    o = lax.map(row_block, jnp.arange(rows))
    return jnp.moveaxis(o, 0, 1).reshape(B, N, H, d)


def _context_attention(q, k, v):
    s = jnp.einsum('bqhd,bkhd->bhqk', q, k).astype(jnp.float32) * (q.shape[-1] ** -0.5)
    p = jax.nn.softmax(s, axis=-1).astype(v.dtype)
    return jnp.einsum('bhqk,bkhd->bqhd', p, v)


def _gdn_chunked(q, k, v, g, beta, s0):
    B, T, H, _ = q.shape
    dv = v.shape[-1]
    nc = T // DN_CHUNK

    def blocks(a):
        return jnp.moveaxis(a.reshape((B, nc, DN_CHUNK, H) + a.shape[3:]), 3, 1)

    q, k, v, beta = blocks(q), blocks(k), blocks(v), blocks(beta)
    g = jnp.cumsum(blocks(g), axis=-1)
    k_beta = k * beta[..., None]
    v_beta = v * beta[..., None]
    idx = jnp.arange(DN_CHUNK)
    lower = idx[:, None] >= idx[None, :]
    strict = idx[:, None] > idx[None, :]
    decay = jnp.exp(jnp.where(lower, g[..., :, None] - g[..., None, :], -jnp.inf))
    a_mat = jnp.where(strict, jnp.einsum('bhncd,bhnsd->bhncs', k_beta, k) * decay, 0.0)
    eye = jnp.eye(DN_CHUNK, dtype=q.dtype)
    t_inv = lax.linalg.triangular_solve(a_mat + eye, jnp.broadcast_to(eye, a_mat.shape),
                                        left_side=True, lower=True, unit_diagonal=True)
    u = t_inv @ v_beta
    w = t_inv @ (k_beta * jnp.exp(g)[..., None])
    intra = jnp.einsum('bhncd,bhnsd->bhncs', q, k) * decay

    def step(S, inp):
        q_i, k_i, u_i, w_i, g_i, intra_i = inp
        v_new = u_i - w_i @ S
        o_i = (q_i * jnp.exp(g_i)[..., None]) @ S + intra_i @ v_new
        g_last = g_i[..., -1:]
        S = S * jnp.exp(g_last)[..., None] + jnp.einsum('bhcd,bhce->bhde', k_i * jnp.exp(g_last - g_i)[..., None], v_new)
        return S, o_i

    xs = (jnp.moveaxis(q, 2, 0), jnp.moveaxis(k, 2, 0), jnp.moveaxis(u, 2, 0),
          jnp.moveaxis(w, 2, 0), jnp.moveaxis(g, 2, 0), jnp.moveaxis(intra, 2, 0))
    s_final, o = lax.scan(step, s0, xs)
    o = jnp.moveaxis(jnp.moveaxis(o, 0, 2), 1, 3).reshape(B, T, H, dv)
    return o, s_final


def _dn_inputs(qkv, a, b, conv_w, a_log, dt_bias, pos):
    B, T, _ = qkv.shape
    qkv = jax.nn.silu(_short_conv(qkv, conv_w)).astype(jnp.float32).reshape(B, T, 3, DN_HEADS, DN_HEAD_DIM)
    q, k, v = _l2norm(qkv[:, :, 0]), _l2norm(qkv[:, :, 1]), qkv[:, :, 2]
    if pos is not None:
        q = _axial_rope(q, pos[0], pos[1])
        k = _axial_rope(k, pos[0], pos[1])
    q = q * DN_HEAD_DIM ** -0.5
    a = a.astype(jnp.float32).reshape(B, T, 2, DN_HEADS)
    b = b.astype(jnp.float32).reshape(B, T, 2, DN_HEADS)
    g = -jnp.exp(a_log.astype(jnp.float32)) * jax.nn.softplus(a + dt_bias.astype(jnp.float32))
    return q, k, v, g, jax.nn.sigmoid(b)


def _bidirectional_gdn(q, k, v, g, beta, s0_f, s0_b):
    flip = lambda t: jnp.flip(t, axis=1)
    o_f, s_f = _gdn_chunked(q, k, v, g[:, :, 0], beta[:, :, 0], s0_f)
    o_b, s_b = _gdn_chunked(flip(q), flip(k), flip(v), flip(g[:, :, 1]), flip(beta[:, :, 1]), s0_b)
    return o_f + flip(o_b), s_f, s_b


def _merge(o_na, o_dn, z, na_out_g, dn_norm_g, w_out):
    B, T = z.shape[0], z.shape[1]
    na = _rmsnorm(o_na.reshape(B, T, NA_WIDTH), na_out_g)
    dn = _rmsnorm(o_dn, dn_norm_g) * jax.nn.silu(z.astype(jnp.float32).reshape(B, T, DN_HEADS, DN_HEAD_DIM))
    return jnp.concatenate([na, dn.reshape(B, T, DN_WIDTH).astype(na.dtype)], -1) @ w_out


def _mixer(hl, hc, w_in, conv_w, a_log, dt_bias, rpb, na_out_g, dn_norm_g, w_out, ctx_out):
    B, N, _ = hl.shape
    splits = [NA_WIDTH, 2 * NA_WIDTH, 3 * NA_WIDTH, 3 * NA_WIDTH + 3 * DN_WIDTH,
              3 * NA_WIDTH + 4 * DN_WIDTH, 3 * NA_WIDTH + 4 * DN_WIDTH + 2 * DN_HEADS]
    qa_l, ka_l, va_l, qkv_l, z_l, b_l, a_l = jnp.split(hl @ w_in, splits, axis=-1)
    qa_c, ka_c, va_c, qkv_c, z_c, b_c, a_c = jnp.split(hc @ w_in, splits, axis=-1)
    heads = lambda t: t.reshape(t.shape[0], t.shape[1], NA_HEADS, NA_HEAD_DIM)
    kc, vc = heads(ka_c), heads(va_c)
    o_na_l = _neighbourhood_attention(heads(qa_l), heads(ka_l), heads(va_l), kc, vc, rpb)
    t = jnp.arange(N)
    q_c, k_c, v_c, g_c, beta_c = _dn_inputs(qkv_c, a_c, b_c, conv_w, a_log, dt_bias, None)
    q_l, k_l, v_l, g_l, beta_l = _dn_inputs(qkv_l, a_l, b_l, conv_w, a_log, dt_bias, (t // GRID_W, t % GRID_W))
    s0 = jnp.zeros((B, DN_HEADS, DN_HEAD_DIM, DN_HEAD_DIM), jnp.float32)
    o_dn_c, s_f, s_b = _bidirectional_gdn(q_c, k_c, v_c, g_c, beta_c, s0, s0)
    o_dn_l, _, _ = _bidirectional_gdn(q_l, k_l, v_l, g_l, beta_l, s_f, s_b)
    y_l = _merge(o_na_l, o_dn_l, z_l, na_out_g, dn_norm_g, w_out)
    if not ctx_out:
        return y_l, None
    o_na_c = _context_attention(heads(qa_c), kc, vc)
    y_c = _merge(o_na_c, o_dn_c, z_c, na_out_g, dn_norm_g, w_out)
    return y_l, y_c


def _swiglu(h, wg, wu, wd):
    return (jax.nn.silu(h @ wg) * (h @ wu)) @ wd


def _moe(h, router, wg, wu, wd):
    logits = (h @ router).astype(jnp.float32)
    top_v, top_i = lax.top_k(logits, TOP_K)
    top_w = jax.nn.softmax(top_v, axis=-1)
    gates = jnp.sum(jax.nn.one_hot(top_i, N_EXPERTS, dtype=jnp.float32) * top_w[..., None], axis=1).astype(h.dtype)
    y = jnp.zeros_like(h)
    for e in range(N_EXPERTS):
        y = y + gates[:, e:e + 1] * _swiglu(h, wg[e], wu[e], wd[e])
    return y


def setup_inputs(seed: int = 0) -> dict:
    key = jax.random.key(seed)
    ks = jax.random.split(key, 32)
    f32 = jnp.float32
    D = D_MODEL

    def nrm(k, shape, s):
        return jax.random.normal(k, shape, f32) * s

    dt = jnp.exp(jax.random.uniform(ks[9], (DEPTH, 2, DN_HEADS), f32, math.log(1e-3), math.log(1e-1)))
    return {
        'x': nrm(ks[0], (BATCH, SEQ, D), 1.0),
        'c': nrm(ks[1], (BATCH, D), 1.0),
        'ctx': nrm(ks[2], (BATCH, CTX_LEN, D), 1.0),
        'c_ctx': nrm(ks[3], (D,), 1.0),
        'w_mod': nrm(ks[4], (DEPTH, D, 6 * D), 0.5 * D ** -0.5),
        'b_mod': nrm(ks[5], (DEPTH, 6 * D), 0.02),
        'w_in': nrm(ks[6], (DEPTH, D, P_IN), D ** -0.5),
        'dn_conv_w': nrm(ks[7], (DEPTH, DN_CONV, 3 * DN_WIDTH), DN_CONV ** -0.5),
        'dn_a_log': jnp.log(jax.random.uniform(ks[8], (DEPTH, 2, DN_HEADS), f32, 1.0, 16.0)),
        'dn_dt_bias': dt + jnp.log(-jnp.expm1(-dt)),
        'dn_norm_g': 1.0 + nrm(ks[10], (DEPTH, DN_HEAD_DIM), 0.02),
        'na_rpb': nrm(ks[11], (DEPTH, NA_HEADS, 2 * NA_KR - 1, 2 * NA_KC - 1), 0.1),
        'na_out_g': 1.0 + nrm(ks[12], (DEPTH, NA_WIDTH), 0.02),
        'w_out': nrm(ks[13], (DEPTH, D_MIX, D), DEEPNORM_BETA * D_MIX ** -0.5),
        'ln1_g': 1.0 + nrm(ks[14], (DEPTH, D), 0.02),
        'ln1_b': nrm(ks[15], (DEPTH, D), 0.02),
        'ln2_g': 1.0 + nrm(ks[16], (DEPTH, D), 0.02),
        'ln2_b': nrm(ks[17], (DEPTH, D), 0.02),
        'ffn_w_gate': nrm(ks[18], (N_DENSE, D, D_FF_DENSE), D ** -0.5),
        'ffn_w_up': nrm(ks[19], (N_DENSE, D, D_FF_DENSE), D ** -0.5),
        'ffn_w_down': nrm(ks[20], (N_DENSE, D_FF_DENSE, D), DEEPNORM_BETA * D_FF_DENSE ** -0.5),
        'moe_router': nrm(ks[21], (N_MOE, D, N_EXPERTS), D ** -0.5),
        'moe_w_gate': nrm(ks[22], (N_MOE, N_EXPERTS, D, D_FF_EXPERT), D ** -0.5),
        'moe_w_up': nrm(ks[23], (N_MOE, N_EXPERTS, D, D_FF_EXPERT), D ** -0.5),
        'moe_w_down': nrm(ks[24], (N_MOE, N_EXPERTS, D_FF_EXPERT, D), DEEPNORM_BETA * D_FF_EXPERT ** -0.5),
    }


def reference(x, c, ctx, c_ctx, w_mod, b_mod, w_in, dn_conv_w, dn_a_log, dn_dt_bias, dn_norm_g, na_rpb,
              na_out_g, w_out, ln1_g, ln1_b, ln2_g, ln2_b, ffn_w_gate, ffn_w_up, ffn_w_down,
              moe_router, moe_w_gate, moe_w_up, moe_w_down):
    L = ctx.shape[1]
    xl, xc = x, ctx
    for l in range(DEPTH):
        ctx_out = l < DEPTH - 1
        m_l = (jax.nn.silu(c) @ w_mod[l] + b_mod[l])[:, None, :]
        m_c = (jax.nn.silu(c_ctx) @ w_mod[l] + b_mod[l])[None, None, :]
        sh1, sc1, g1, sh2, sc2, g2 = jnp.split(m_l, 6, axis=-1)
        csh1, csc1, cg1, csh2, csc2, cg2 = jnp.split(m_c, 6, axis=-1)
        y_l, y_c = _mixer(_modulate(xl, sh1, sc1), _modulate(xc, csh1, csc1), w_in[l], dn_conv_w[l],
                          dn_a_log[l], dn_dt_bias[l], na_rpb[l], na_out_g[l], dn_norm_g[l], w_out[l], ctx_out)
        xl = _layernorm(DEEPNORM_ALPHA * xl + g1 * y_l, ln1_g[l], ln1_b[l])
        if ctx_out:
            xc = _layernorm(DEEPNORM_ALPHA * xc + cg1 * y_c, ln1_g[l], ln1_b[l])
        h = _modulate(xl, sh2, sc2)
        if ctx_out:
            h = jnp.concatenate([_modulate(xc, csh2, csc2), h], axis=1)
        hf = h.reshape(-1, h.shape[-1])
        i = l // 2
        if l % 2 == 0:
            f = _swiglu(hf, ffn_w_gate[i], ffn_w_up[i], ffn_w_down[i])
        else:
            f = _moe(hf, moe_router[i], moe_w_gate[i], moe_w_up[i], moe_w_down[i])
        f = f.reshape(h.shape)
        if ctx_out:
            xc = _layernorm(DEEPNORM_ALPHA * xc + cg2 * f[:, :L], ln2_g[l], ln2_b[l])
            f = f[:, L:]
        xl = _layernorm(DEEPNORM_ALPHA * xl + g2 * f, ln2_g[l], ln2_b[l])
    return xl
```

```python
import functools

import jax
import jax.numpy as jnp
from jax import lax
from jax.experimental import pallas as pl
from jax.experimental.pallas import tpu as pltpu

F32 = jnp.float32
BF16 = jnp.bfloat16
HIGHEST = lax.Precision.HIGHEST

D_MODEL = 1024
GRID_W = 64
CTX_LEN = 256
NA_HEADS = 8
NA_HEAD_DIM = 64
NA_WIDTH = NA_HEADS * NA_HEAD_DIM
NA_KR = 8
NA_KC = 16
DN_HEAD_DIM = 128
DN_HEADS = 4
DN_WIDTH = DN_HEADS * DN_HEAD_DIM
DN_CONV = 5
CHUNK = 64
ROPE_THETA = 10000.0
N_EXPERTS = 8
TOP_K = 2
LN_EPS = 1e-5
RMS_EPS = 1e-6
NEG_BIG = -1e30

TILE = 256
CHUNKS_PER_TILE = TILE // CHUNK
ROWS_PER_TILE = TILE // GRID_W
LANES = 128
VMEM_LIMIT = 56 * 1024 * 1024
FFN_TM = 1024
FFN_SUB = 256


def _params(n_axes, vmem=VMEM_LIMIT):
    return pltpu.CompilerParams(dimension_semantics=("arbitrary",) * n_axes, vmem_limit_bytes=vmem)


def _dot(a, b, precision=None):
    return jnp.dot(a, b, preferred_element_type=F32, precision=precision)


def _dot_nt(a, b):
    return lax.dot_general(a, b, (((1,), (1,)), ((), ())), preferred_element_type=F32)


def _dot_tn(a, b):
    return lax.dot_general(a, b, (((0,), (0,)), ((), ())), preferred_element_type=F32)


def _silu(x):
    return x * jax.nn.sigmoid(x)


def _softplus(x):
    return jnp.maximum(x, 0.0) + jnp.log1p(jnp.exp(-jnp.abs(x)))


def _mod_spec(k, nb):
    return pl.BlockSpec((1, 1, D_MODEL), lambda b, t: (jnp.where(t == 0, nb, b), 0, k))


def _mod_kernel(c_ref, w_ref, b_ref, o_ref):
    o_ref[0] = _dot(_silu(c_ref[...]), w_ref[0], HIGHEST) + b_ref[0]


def _modulation(c_all, w_mod, b_mod):
    depth, _, n = w_mod.shape
    rows = c_all.shape[0]
    tn = 1536
    return pl.pallas_call(
        _mod_kernel,
        grid=(depth, n // tn),
        in_specs=[pl.BlockSpec((rows, D_MODEL), lambda l, j: (0, 0)),
                  pl.BlockSpec((1, D_MODEL, tn), lambda l, j: (l, 0, j)),
                  pl.BlockSpec((1, 1, tn), lambda l, j: (l, 0, j))],
        out_specs=pl.BlockSpec((1, rows, tn), lambda l, j: (l, 0, j)),
        out_shape=jax.ShapeDtypeStruct((depth, rows, n), F32),
        compiler_params=_params(2),
        name="modulation",
    )(c_all, w_mod, b_mod.reshape(depth, 1, n))


def _inproj_kernel(x_ref, sh_ref, sc_ref, w_ref, na_ref, dn_ref, z_ref, ba_ref):
    h = (x_ref[0] * (1.0 + sc_ref[0]) + sh_ref[0]).astype(BF16)
    na = _dot(h, w_ref[:, 0:3 * NA_WIDTH])
    col = lax.broadcasted_iota(jnp.int32, (1, 3 * NA_WIDTH), 1)
    na_ref[0] = (na * jnp.where(col < NA_WIDTH, NA_HEAD_DIM ** -0.5, 1.0)).astype(BF16)
    o = 3 * NA_WIDTH
    dn_ref[0] = _dot(h, w_ref[:, o:o + 3 * DN_WIDTH])
    o += 3 * DN_WIDTH
    z_ref[0] = _dot(h, w_ref[:, o:o + DN_WIDTH])
    o += DN_WIDTH
    ba_ref[0] = _dot(h, w_ref[:, o:o + LANES])


def _in_projection(xs, mods, w_in_p):
    nb, tt, _ = xs.shape
    nt = tt // TILE
    tok = lambda w: pl.BlockSpec((1, TILE, w), lambda b, t: (b, t, 0))
    return pl.pallas_call(
        _inproj_kernel,
        grid=(nb, nt),
        in_specs=[tok(D_MODEL), _mod_spec(0, nb), _mod_spec(1, nb),
                  pl.BlockSpec(w_in_p.shape, lambda b, t: (0, 0))],
        out_specs=[tok(3 * NA_WIDTH), tok(3 * DN_WIDTH), tok(DN_WIDTH), tok(LANES)],
        out_shape=[jax.ShapeDtypeStruct((nb, tt, 3 * NA_WIDTH), BF16),
                   jax.ShapeDtypeStruct((nb, tt, 3 * DN_WIDTH), F32),
                   jax.ShapeDtypeStruct((nb, tt, DN_WIDTH), F32),
                   jax.ShapeDtypeStruct((nb, tt, LANES), F32)],
        compiler_params=_params(2),
        name="in_projection",
    )(xs, mods, mods, w_in_p)


def _na_bias_table(rpb):
    qc = jnp.arange(GRID_W)[:, None]
    kc = jnp.arange(GRID_W)[None, :]
    cstart = jnp.clip(qc - NA_KC // 2, 0, GRID_W - NA_KC)
    ok = (kc >= cstart) & (kc < cstart + NA_KC)
    dc = jnp.clip(kc - qc + NA_KC - 1, 0, 2 * NA_KC - 2)
    dr = jnp.arange(NA_KR)[None, :] - jnp.arange(NA_KR)[:, None] + NA_KR - 1
    tab = rpb[:, dr[:, :, None, None], dc[None, None, :, :]]
    tab = jnp.where(ok[None, None, None], tab, NEG_BIG)
    return tab.transpose(1, 0, 3, 2, 4).reshape(NA_KR, NA_HEADS, GRID_W, NA_KR * GRID_W)


def _attn_kernel(q_ref, k_ref, v_ref, bias_ref, o_ref):
    t = pl.program_id(1)
    lo = lax.broadcasted_iota(jnp.int32, (1, LANES), 1) < NA_HEAD_DIM
    n_rows = (k_ref.shape[1] - CTX_LEN) // GRID_W

    def head_pair(qp, kcp, vcp, kp, vp, bias):
        outs = []
        for half in range(2):
            qm = jnp.where(lo if half == 0 else jnp.logical_not(lo), qp, jnp.zeros_like(qp))
            s_c = _dot_nt(qm, kcp)
            mx = jnp.max(s_c, axis=-1, keepdims=True)
            if kp is not None:
                s_l = _dot_nt(qm, kp) + bias(half)
                mx = jnp.maximum(mx, jnp.max(s_l, axis=-1, keepdims=True))
                p_l = jnp.exp(s_l - mx)
            p_c = jnp.exp(s_c - mx)
            den = jnp.sum(p_c, axis=-1, keepdims=True)
            o = _dot(p_c.astype(BF16), vcp)
            if kp is not None:
                den = den + jnp.sum(p_l, axis=-1, keepdims=True)
                o = o + _dot(p_l.astype(BF16), vp)
            outs.append(o / den)
        return jnp.where(lo, outs[0], outs[1])

    @pl.when(t == 0)
    def _context():
        for pair in range(NA_HEADS // 2):
            ls = slice(LANES * pair, LANES * (pair + 1))
            o_ref[0, :, ls] = head_pair(q_ref[0, :, ls], k_ref[0, 0:CTX_LEN, ls], v_ref[0, 0:CTX_LEN, ls],
                                        None, None, None)

    @pl.when(t > 0)
    def _latent():
        def row_body(rr, carry):
            r = (t - 1) * ROWS_PER_TILE + rr
            rs = jnp.clip(r - NA_KR // 2, 0, n_rows - NA_KR)
            variant = r - rs
            start = pl.multiple_of(CTX_LEN + rs * GRID_W, GRID_W)
            qrows = pl.ds(pl.multiple_of(rr * GRID_W, GRID_W), GRID_W)
            for pair in range(NA_HEADS // 2):
                ls = slice(LANES * pair, LANES * (pair + 1))
                o_ref[0, qrows, ls] = head_pair(
                    q_ref[0, qrows, ls], k_ref[0, 0:CTX_LEN, ls], v_ref[0, 0:CTX_LEN, ls],
                    k_ref[0, pl.ds(start, NA_KR * GRID_W), ls], v_ref[0, pl.ds(start, NA_KR * GRID_W), ls],
                    lambda half: bias_ref[variant, 2 * pair + half])
            return carry
        lax.fori_loop(0, ROWS_PER_TILE, row_body, 0)


def _attention(qkv, bias_tab):
    nb, tt, _ = qkv.shape
    nt = tt // TILE
    return pl.pallas_call(
        _attn_kernel,
        grid=(nb, nt),
        in_specs=[pl.BlockSpec((1, TILE, NA_WIDTH), lambda b, t: (b, t, 0)),
                  pl.BlockSpec((1, tt, NA_WIDTH), lambda b, t: (b, 0, 1)),
                  pl.BlockSpec((1, tt, NA_WIDTH), lambda b, t: (b, 0, 2)),
                  pl.BlockSpec(bias_tab.shape, lambda b, t: (0, 0, 0, 0))],
        out_specs=pl.BlockSpec((1, TILE, NA_WIDTH), lambda b, t: (b, t, 0)),
        out_shape=jax.ShapeDtypeStruct((nb, tt, NA_WIDTH), F32),
        compiler_params=_params(2),
        name="attention",
    )(qkv, qkv, qkv, bias_tab)


def _rope_tables(seq):
    nf = DN_HEAD_DIM // 4
    inv = ROPE_THETA ** (-jnp.arange(nf, dtype=F32) / nf)
    t = jnp.arange(seq)
    lane = jnp.arange(DN_HEAD_DIM)
    pos = jnp.where(lane[None, :] < DN_HEAD_DIM // 2, (t // GRID_W)[:, None], (t % GRID_W)[:, None]).astype(F32)
    ang = pos * inv[lane % nf][None, :]
    first = (lane % (2 * nf)) < nf
    cos = jnp.cos(ang)
    sin = jnp.where(first[None, :], -jnp.sin(ang), jnp.sin(ang))
    cos = jnp.concatenate([jnp.ones((CTX_LEN, DN_HEAD_DIM), F32), cos], 0)
    sin = jnp.concatenate([jnp.zeros((CTX_LEN, DN_HEAD_DIM), F32), sin], 0)
    return cos, sin


def _dnprep_kernel(cur_ref, prev_ref, next_ref, cw_ref, cos_ref, sin_ref, q_ref, k_ref, v_ref, pad_ref):
    t = pl.program_id(1)
    nt = pl.num_programs(1)
    halo = prev_ref.shape[1]
    prev_ok = t >= 2
    next_ok = jnp.logical_and(t >= 1, t < nt - 1)
    pad_ref[0:halo] = jnp.where(prev_ok, prev_ref[0], 0.0)
    pad_ref[halo:halo + TILE] = cur_ref[0]
    pad_ref[halo + TILE:2 * halo + TILE] = jnp.where(next_ok, next_ref[0], 0.0)
    lane = lax.broadcasted_iota(jnp.int32, (1, LANES), 1)
    first = (lane % (DN_HEAD_DIM // 2)) < DN_HEAD_DIM // 4
    cos = cos_ref[...]
    sin = sin_ref[...]
    outs = (q_ref, k_ref, v_ref)
    for grp in range(3 * DN_HEADS):
        ls = slice(LANES * grp, LANES * (grp + 1))
        acc = None
        for j in range(DN_CONV):
            term = cw_ref[j:j + 1, ls] * pad_ref[pl.ds(halo - DN_CONV // 2 + j, TILE), ls]
            acc = term if acc is None else acc + term
        x = _silu(acc)
        which, head = divmod(grp, DN_HEADS)
        if which < 2:
            x = x * lax.rsqrt(jnp.sum(x * x, axis=-1, keepdims=True) + RMS_EPS)
            rot = jnp.where(first, pltpu.roll(x, LANES - DN_HEAD_DIM // 4, 1), pltpu.roll(x, DN_HEAD_DIM // 4, 1))
            x = x * cos + rot * sin
        if which == 0:
            x = x * DN_HEAD_DIM ** -0.5
        outs[which][0, :, LANES * head:LANES * (head + 1)] = x


def _dn_prepare(qkv_dn, conv_w, cos, sin):
    nb, tt, w = qkv_dn.shape
    nt = tt // TILE
    halo = 8
    hpt = TILE // halo
    cw = jnp.zeros((8, w), F32).at[:DN_CONV].set(conv_w)
    out = pl.BlockSpec((1, TILE, DN_WIDTH), lambda b, t: (b, t, 0))
    return pl.pallas_call(
        _dnprep_kernel,
        grid=(nb, nt),
        in_specs=[pl.BlockSpec((1, TILE, w), lambda b, t: (b, t, 0)),
                  pl.BlockSpec((1, halo, w), lambda b, t: (b, jnp.maximum(t * hpt - 1, 0), 0)),
                  pl.BlockSpec((1, halo, w), lambda b, t: (b, jnp.minimum((t + 1) * hpt, nt * hpt - 1), 0)),
                  pl.BlockSpec((8, w), lambda b, t: (0, 0)),
                  pl.BlockSpec((TILE, DN_HEAD_DIM), lambda b, t: (t, 0)),
                  pl.BlockSpec((TILE, DN_HEAD_DIM), lambda b, t: (t, 0))],
        out_specs=[out, out, out],
        out_shape=[jax.ShapeDtypeStruct((nb, tt, DN_WIDTH), F32)] * 3,
        scratch_shapes=[pltpu.VMEM((TILE + 2 * halo, w), F32)],
        compiler_params=_params(2),
        name="dn_prepare",
    )(qkv_dn, qkv_dn, qkv_dn, cw, cos, sin)


def _gate_kernel(a_ref, b_ref, alog_ref, dtb_ref, g_ref, beta_ref):
    g = -jnp.exp(alog_ref[...]) * _softplus(a_ref[0] + dtb_ref[...])
    beta_ref[0] = jax.nn.sigmoid(b_ref[0])
    ii = lax.broadcasted_iota(jnp.int32, (CHUNK, CHUNK), 0)
    jj = lax.broadcasted_iota(jnp.int32, (CHUNK, CHUNK), 1)
    g_ref[0, 0] = _dot(g[0], (ii <= jj).astype(F32), HIGHEST)
    g_ref[0, 1] = _dot(g[1], (ii >= jj).astype(F32), HIGHEST)


def _gates(ba, a_log, dt_bias):
    nb, tt, _ = ba.shape
    nc = tt // CHUNK
    nh = DN_HEADS
    rows = lambda x: x.transpose(0, 2, 1).reshape(nb, 2, nh * nc, CHUNK)
    spec = pl.BlockSpec((1, 2, nh * nc, CHUNK), lambda b: (b, 0, 0, 0))
    par = lambda p: jnp.broadcast_to(p.reshape(2, nh, 1, 1).astype(F32), (2, nh, nc, CHUNK)).reshape(2, nh * nc, CHUNK)
    pspec = pl.BlockSpec((2, nh * nc, CHUNK), lambda b: (0, 0, 0))
    gcum, beta = pl.pallas_call(
        _gate_kernel,
        grid=(nb,),
        in_specs=[spec, spec, pspec, pspec],
        out_specs=[spec, spec],
        out_shape=[jax.ShapeDtypeStruct((nb, 2, nh * nc, CHUNK), F32)] * 2,
        compiler_params=_params(1),
        name="dn_gates",
    )(rows(ba[..., 2 * nh:4 * nh]), rows(ba[..., 0:2 * nh]), par(a_log), par(dt_bias))
    return gcum.reshape(nb, 2 * nh, nc, CHUNK), beta.reshape(nb, 2 * nh, nc, CHUNK)


def _gdn_parallel(q, k, v, col, row, d):
    nh = DN_HEADS
    stack = lambda x: jnp.concatenate([x[:, LANES * h:LANES * (h + 1)] for h in range(nh)], axis=0)
    colb = lambda c, w: jnp.broadcast_to(col[:, c:c + 1], (CHUNK, w))
    ks, qs, vs = stack(k), stack(q), stack(v)
    gs = jnp.concatenate([colb(nh * d + h, LANES) for h in range(nh)], axis=0)
    bs = jnp.concatenate([colb(2 * nh + nh * d + h, LANES) for h in range(nh)], axis=0)
    eg = jnp.exp(gs)
    kbs = ks * bs
    rhs = jnp.concatenate([vs * bs, kbs * eg], axis=1).astype(BF16)
    ksb = ks.astype(BF16)
    full_kk = _dot_nt(kbs.astype(BF16), ksb)
    full_qk = _dot_nt(qs.astype(BF16), ksb)

    width = nh * CHUNK
    lane = lax.broadcasted_iota(jnp.int32, (CHUNK, width), 1)
    ri = lax.broadcasted_iota(jnp.int32, (CHUNK, width), 0)
    jj = lane % CHUNK
    hb = lane // CHUNK
    lower = (ri >= jj) if d == 0 else (ri <= jj)
    strict = (ri > jj) if d == 0 else (ri < jj)

    def pack(full):
        acc = full[0:CHUNK]
        for h in range(1, nh):
            acc = jnp.where(hb == h, full[CHUNK * h:CHUNK * (h + 1)], acc)
        return acc

    def blockdiag(xp):
        return jnp.concatenate([jnp.where(hb == h, xp, 0.0) for h in range(nh)], axis=0).astype(BF16)

    cp = colb(nh * d, width)
    for h in range(1, nh):
        cp = jnp.where(hb == h, colb(nh * d + h, width), cp)
    decay = jnp.exp(jnp.where(lower, cp - row, NEG_BIG))
    a_p = jnp.where(strict, pack(full_kk) * decay, 0.0)
    intra = pack(full_qk) * decay

    m_p = -a_p
    p_p = (ri == jj).astype(F32) + m_p
    span = 2
    while span < CHUNK:
        m_p = _dot(m_p.astype(BF16), blockdiag(m_p))
        p_p = p_p + _dot(p_p.astype(BF16), blockdiag(m_p))
        span *= 2
    uw = _dot(blockdiag(p_p), rhs)
    return dict(u=uw[:, 0:LANES], w=uw[:, LANES:2 * LANES], qg=qs * eg, k=ks, g=gs, intra=blockdiag(intra))


def _gdn_recurrent(p, s_ref, d, o_ref, rows):
    nh = DN_HEADS
    vnew, o_state = [], []
    for h in range(nh):
        c = nh * d + h
        hs = slice(CHUNK * h, CHUNK * (h + 1))
        s = s_ref[c]
        r = _dot(jnp.concatenate([p["w"][hs], p["qg"][hs]], axis=0).astype(BF16), s.astype(BF16))
        vn = p["u"][hs] - r[0:CHUNK]
        g = p["g"][hs]
        g_last = g[CHUNK - 1:CHUNK] if d == 0 else g[0:1]
        kg = (p["k"][hs] * jnp.exp(g_last - g)).astype(BF16)
        s_ref[c] = s * jnp.exp(g_last) + _dot_tn(kg, vn.astype(BF16))
        vnew.append(vn)
        o_state.append(r[CHUNK:2 * CHUNK])
    o_intra = _dot(p["intra"], jnp.concatenate(vnew, axis=0).astype(BF16))
    for h in range(nh):
        o_ref[0, rows, LANES * h:LANES * (h + 1)] = o_state[h] + o_intra[CHUNK * h:CHUNK * (h + 1)]


def _gdn_kernel(qf, kf, vf, colf, rowf, qb, kb, vb, colb, rowb, of_ref, ob_ref, s_ref):
    @pl.when(pl.program_id(1) == 0)
    def _reset():
        s_ref[...] = jnp.zeros_like(s_ref)

    srcs = ((qf, kf, vf, colf, rowf, of_ref), (qb, kb, vb, colb, rowb, ob_ref))
    order = lambda d, step: step if d == 0 else CHUNKS_PER_TILE - 1 - step
    par = {}
    for step in range(CHUNKS_PER_TILE):
        for d in range(2):
            q, k, v, col, row, _ = srcs[d]
            ch = order(d, step)
            rows = slice(CHUNK * ch, CHUNK * (ch + 1))
            par[d, step] = _gdn_parallel(q[0, rows, :], k[0, rows, :], v[0, rows, :], col[0, rows, :],
                                         row[0, ch, d:d + 1, :], d)
    for step in range(CHUNKS_PER_TILE):
        for d in range(2):
            ch = order(d, step)
            _gdn_recurrent(par[d, step], s_ref, d, srcs[d][5], slice(CHUNK * ch, CHUNK * (ch + 1)))


def _gdn(q, k, v, gcum, beta):
    nb, tt, _ = q.shape
    nt = tt // TILE
    nh = DN_HEADS
    col = jnp.concatenate([gcum, beta], axis=1).transpose(0, 2, 3, 1).reshape(nb, tt, 4 * nh)
    row = gcum.reshape(nb, 2, nh, tt // CHUNK, CHUNK).transpose(0, 3, 1, 2, 4).reshape(nb, tt // CHUNK, 2, nh * CHUNK)
    fwd = lambda b, t: (b, t, 0)
    bwd = lambda b, t: (b, jnp.where(t == 0, 0, nt - t), 0)
    tok = lambda w, im: pl.BlockSpec((1, TILE, w), im)
    rowspec = lambda im: pl.BlockSpec((1, CHUNKS_PER_TILE, 2, nh * CHUNK), lambda b, t: im(b, t) + (0,))
    side = lambda im: [tok(DN_WIDTH, im), tok(DN_WIDTH, im), tok(DN_WIDTH, im), tok(4 * nh, im), rowspec(im)]
    return pl.pallas_call(
        _gdn_kernel,
        grid=(nb, nt),
        in_specs=side(fwd) + side(bwd),
        out_specs=[tok(DN_WIDTH, fwd), tok(DN_WIDTH, bwd)],
        out_shape=[jax.ShapeDtypeStruct((nb, tt, DN_WIDTH), F32)] * 2,
        scratch_shapes=[pltpu.VMEM((2 * nh, DN_HEAD_DIM, DN_HEAD_DIM), F32)],
        compiler_params=_params(2),
        name="gated_deltanet",
    )(q, k, v, col, row, q, k, v, col, row)


def _layernorm(x, g, b):
    mu = jnp.mean(x, axis=-1, keepdims=True)
    xc = x - mu
    var = jnp.mean(xc * xc, axis=-1, keepdims=True)
    return xc * lax.rsqrt(var + LN_EPS) * g + b


def _merge_kernel(alpha, with_router, ona_ref, of_ref, ob_ref, z_ref, x_ref, g1_ref, sh2_ref, sc2_ref,
                  nag_ref, dng_ref, wout_ref, lng_ref, lnb_ref, *rest):
    if with_router:
        router_ref, x1_ref, h2_ref, logit_ref = rest
    else:
        x1_ref, h2_ref = rest
    ona = ona_ref[0]
    na = ona * lax.rsqrt(jnp.mean(ona * ona, axis=-1, keepdims=True) + RMS_EPS) * nag_ref[...]
    y = _dot(na.astype(BF16), wout_ref[0:NA_WIDTH, :])
    for h in range(DN_HEADS):
        ls = slice(LANES * h, LANES * (h + 1))
        od = of_ref[0, :, ls] + ob_ref[0, :, ls]
        dn = od * lax.rsqrt(jnp.mean(od * od, axis=-1, keepdims=True) + RMS_EPS) * dng_ref[...]
        dn = dn * _silu(z_ref[0, :, ls])
        y = y + _dot(dn.astype(BF16), wout_ref[NA_WIDTH + LANES * h:NA_WIDTH + LANES * (h + 1), :])
    x1 = _layernorm(alpha * x_ref[0] + g1_ref[0] * y, lng_ref[...], lnb_ref[...])
    x1_ref[0] = x1
    h2 = x1 * (1.0 + sc2_ref[0]) + sh2_ref[0]
    h2_ref[0] = h2.astype(BF16)
    if with_router:
        logit_ref[0] = _dot(h2, router_ref[...], HIGHEST)


def _merge(alpha, o_na, o_f, o_b, z, xs, mods, na_g, dn_g, w_out, ln_g, ln_b, router_p):
    nb, tt, _ = xs.shape
    nt = tt // TILE
    tok = lambda w: pl.BlockSpec((1, TILE, w), lambda b, t: (b, t, 0))
    vec = lambda w: pl.BlockSpec((1, w), lambda b, t: (0, 0))
    with_router = router_p is not None
    in_specs = [tok(NA_WIDTH), tok(DN_WIDTH), tok(DN_WIDTH), tok(DN_WIDTH), tok(D_MODEL),
                _mod_spec(2, nb), _mod_spec(3, nb), _mod_spec(4, nb), vec(NA_WIDTH), vec(DN_HEAD_DIM),
                pl.BlockSpec(w_out.shape, lambda b, t: (0, 0)), vec(D_MODEL), vec(D_MODEL)]
    args = [o_na, o_f, o_b, z, xs, mods, mods, mods, na_g.reshape(1, -1), dn_g.reshape(1, -1), w_out,
            ln_g.reshape(1, -1), ln_b.reshape(1, -1)]
    out_specs = [tok(D_MODEL), tok(D_MODEL)]
    out_shape = [jax.ShapeDtypeStruct((nb, tt, D_MODEL), F32), jax.ShapeDtypeStruct((nb, tt, D_MODEL), BF16)]
    if with_router:
        in_specs.append(pl.BlockSpec(router_p.shape, lambda b, t: (0, 0)))
        args.append(router_p)
        out_specs.append(tok(LANES))
        out_shape.append(jax.ShapeDtypeStruct((nb, tt, LANES), F32))
    return pl.pallas_call(
        functools.partial(_merge_kernel, alpha, with_router),
        grid=(nb, nt),
        in_specs=in_specs,
        out_specs=out_specs,
        out_shape=out_shape,
        compiler_params=_params(2),
        name="merge_router" if with_router else "merge",
    )(*args)


def _resid_ln_kernel(alpha, x_ref, f_ref, g2_ref, lng_ref, lnb_ref, o_ref):
    o_ref[0] = _layernorm(alpha * x_ref[0] + g2_ref[0] * f_ref[0], lng_ref[...], lnb_ref[...])


def _resid_ln(alpha, xs, f, mods, ln_g, ln_b):
    nb, tt, _ = xs.shape
    tok = pl.BlockSpec((1, TILE, D_MODEL), lambda b, t: (b, t, 0))
    vec = pl.BlockSpec((1, D_MODEL), lambda b, t: (0, 0))
    return pl.pallas_call(
        functools.partial(_resid_ln_kernel, alpha),
        grid=(nb, tt // TILE),
        in_specs=[tok, tok, _mod_spec(5, nb), vec, vec],
        out_specs=tok,
        out_shape=jax.ShapeDtypeStruct(xs.shape, F32),
        compiler_params=_params(2),
        name="residual_layernorm",
    )(xs, f, mods, ln_g.reshape(1, -1), ln_b.reshape(1, -1))


def _ffn_kernel(te_ref, x_ref, gate_ref, wg_ref, wu_ref, wd_ref, o_ref, act_ref):
    del te_ref
    f = pl.program_id(1)
    x = x_ref[...]
    tf = act_ref.shape[1]
    off = 0
    while off < tf:
        n = min(FFN_SUB, tf - off)
        gt = _dot(x, wg_ref[0, :, off:off + n])
        up = _dot(x, wu_ref[0, :, off:off + n])
        act_ref[:, off:off + n] = (_silu(gt) * up).astype(BF16)
        off += n
    part = _dot(act_ref[...], wd_ref[0])

    @pl.when(f == 0)
    def _first():
        o_ref[...] = part

    @pl.when(f > 0)
    def _rest():
        o_ref[...] += part

    @pl.when(f == pl.num_programs(1) - 1)
    def _gate():
        o_ref[...] = o_ref[...] * gate_ref[...]


def _ffn(x, gate, tile_expert, wg, wu, wd, tm, tf):
    s, dm = x.shape
    ff = wg.shape[2]
    grid_spec = pltpu.PrefetchScalarGridSpec(
        num_scalar_prefetch=1,
        grid=(s // tm, ff // tf),
        in_specs=[pl.BlockSpec((tm, dm), lambda i, f, te: (i, 0)),
                  pl.BlockSpec((tm, 1), lambda i, f, te: (i, 0)),
                  pl.BlockSpec((1, dm, tf), lambda i, f, te: (te[i], 0, f)),
                  pl.BlockSpec((1, dm, tf), lambda i, f, te: (te[i], 0, f)),
                  pl.BlockSpec((1, tf, dm), lambda i, f, te: (te[i], f, 0))],
        out_specs=pl.BlockSpec((tm, dm), lambda i, f, te: (i, 0)),
        scratch_shapes=[pltpu.VMEM((tm, tf), BF16)],
    )
    return pl.pallas_call(
        _ffn_kernel,
        grid_spec=grid_spec,
        out_shape=jax.ShapeDtypeStruct((s, dm), F32),
        compiler_params=_params(2),
        name="swiglu",
    )(tile_expert, x, gate, wg, wu, wd)


def _dense_ffn(h2, wg, wu, wd):
    nb, tt, dm = h2.shape
    s = nb * tt
    tm = FFN_TM if s % FFN_TM == 0 else TILE
    ff = wg.shape[1]
    tf = ff // 2 if (ff // 2) % LANES == 0 else ff
    out = _ffn(h2.reshape(s, dm), jnp.ones((s, 1), F32), jnp.zeros((s // tm,), jnp.int32),
               wg[None], wu[None], wd[None], tm, tf)
    return out.reshape(nb, tt, dm)


def _moe_ffn(h2, logits, wg, wu, wd):
    nb, tt, dm = h2.shape
    s = nb * tt
    tm = FFN_TM
    ne = N_EXPERTS
    top_v, top_i = lax.top_k(logits.reshape(s, -1)[:, :ne], TOP_K)
    top_w = jax.nn.softmax(top_v, axis=-1)
    e_flat = top_i.reshape(-1)
    onehot = (e_flat[:, None] == jnp.arange(ne)[None, :]).astype(jnp.int32)
    csum = jnp.cumsum(onehot, axis=0)
    counts = csum[-1]
    rank = jnp.take_along_axis(csum, e_flat[:, None], axis=1)[:, 0] - 1
    padded = ((counts + tm - 1) // tm) * tm
    ends = jnp.cumsum(padded)
    slot = (ends - padded)[e_flat] + rank
    n_tiles = (TOP_K * s + ne * (tm - 1)) // tm
    s_pad = n_tiles * tm
    src = jnp.zeros((s_pad,), jnp.int32).at[slot].set(jnp.arange(TOP_K * s, dtype=jnp.int32) // TOP_K)
    gate = jnp.zeros((s_pad,), F32).at[slot].set(top_w.reshape(-1))
    tile_expert = jnp.minimum(jnp.sum((jnp.arange(n_tiles) * tm)[:, None] >= ends[None, :], axis=1), ne - 1)
    xs = jnp.take(h2.reshape(s, dm), src, axis=0)
    ff = wg.shape[2]
    out = _ffn(xs, gate[:, None], tile_expert.astype(jnp.int32), wg, wu, wd, tm, ff // 2)
    f = jnp.take(out, slot, axis=0).reshape(s, TOP_K, dm).sum(axis=1)
    return f.reshape(nb, tt, dm)


def kernel(x, c, ctx, c_ctx, w_mod, b_mod, w_in, dn_conv_w, dn_a_log, dn_dt_bias, dn_norm_g, na_rpb, na_out_g, w_out, ln1_g, ln1_b, ln2_g, ln2_b, ffn_w_gate, ffn_w_up, ffn_w_down, moe_router, moe_w_gate, moe_w_up, moe_w_down):
    nb, seq, dm = x.shape
    depth = w_mod.shape[0]
    alpha = (2.0 * depth) ** 0.25
    n_ctx = ctx.shape[1]
    assert n_ctx == CTX_LEN == TILE and seq % TILE == 0 and dm == D_MODEL

    xs = jnp.concatenate([ctx, x], axis=1)
    rows = -(-(nb + 1) // 8) * 8
    c_all = jnp.zeros((rows, dm), F32).at[:nb].set(c).at[nb].set(c_ctx)
    mods_all = _modulation(c_all, w_mod, b_mod)
    cos, sin = _rope_tables(seq)
    p_in = w_in.shape[2]
    p_pad = 3 * NA_WIDTH + 4 * DN_WIDTH + LANES

    for l in range(depth):
        mods = mods_all[l].reshape(rows, 1, 6 * dm)
        w_in_p = jnp.zeros((dm, p_pad), BF16).at[:, :p_in].set(w_in[l].astype(BF16))
        qkv_na, qkv_dn, z, ba = _in_projection(xs, mods, w_in_p)
        o_na = _attention(qkv_na, _na_bias_table(na_rpb[l]))
        q_dn, k_dn, v_dn = _dn_prepare(qkv_dn, dn_conv_w[l], cos, sin)
        gcum, beta = _gates(ba, dn_a_log[l], dn_dt_bias[l])
        o_f, o_b = _gdn(q_dn, k_dn, v_dn, gcum, beta)
        moe = l % 2 == 1
        i = l // 2
        router_p = None
        if moe:
            router_p = jnp.zeros((dm, LANES), F32).at[:, :N_EXPERTS].set(moe_router[i])
        merged = _merge(alpha, o_na, o_f, o_b, z, xs, mods, na_out_g[l], dn_norm_g[l], w_out[l].astype(BF16),
                        ln1_g[l], ln1_b[l], router_p)
        if moe:
            x1, h2, logits = merged
            f = _moe_ffn(h2, logits, moe_w_gate[i].astype(BF16), moe_w_up[i].astype(BF16),
                         moe_w_down[i].astype(BF16))
        else:
            x1, h2 = merged
            f = _dense_ffn(h2, ffn_w_gate[i].astype(BF16), ffn_w_up[i].astype(BF16), ffn_w_down[i].astype(BF16))
        xs = _resid_ln(alpha, x1, f, mods, ln2_g[l], ln2_b[l])
    return xs[:, n_ctx:]
```

```python
import functools

import jax
import jax.numpy as jnp
from jax import lax
from jax.experimental import pallas as pl
from jax.experimental.pallas import tpu as pltpu

F32 = jnp.float32
BF16 = jnp.bfloat16
HIGHEST = lax.Precision.HIGHEST

D_MODEL = 1024
GRID_W = 64
CTX_LEN = 256
NA_HEADS = 8
NA_HEAD_DIM = 64
NA_WIDTH = NA_HEADS * NA_HEAD_DIM
NA_KR = 8
NA_KC = 16
DN_HEAD_DIM = 128
DN_HEADS = 4
DN_WIDTH = DN_HEADS * DN_HEAD_DIM
DN_CONV = 5
CHUNK = 64
ROPE_THETA = 10000.0
N_EXPERTS = 8
TOP_K = 2
LN_EPS = 1e-5
RMS_EPS = 1e-6
NEG_BIG = -1e30

TILE = 256
CHUNKS_PER_TILE = TILE // CHUNK
ROWS_PER_TILE = TILE // GRID_W
LANES = 128
VMEM_LIMIT = 56 * 1024 * 1024
FFN_TM = 1024
FFN_SUB = 256


def _params(n_axes, vmem=VMEM_LIMIT):
    return pltpu.CompilerParams(dimension_semantics=("arbitrary",) * n_axes, vmem_limit_bytes=vmem)


def _dot(a, b, precision=None):
    return jnp.dot(a, b, preferred_element_type=F32, precision=precision)


def _dot_nt(a, b):
    return lax.dot_general(a, b, (((1,), (1,)), ((), ())), preferred_element_type=F32)


def _dot_tn(a, b):
    return lax.dot_general(a, b, (((0,), (0,)), ((), ())), preferred_element_type=F32)


def _silu(x):
    return x * jax.nn.sigmoid(x)


def _softplus(x):
    return jnp.maximum(x, 0.0) + jnp.log1p(jnp.exp(-jnp.abs(x)))


def _mod_spec(k, nb):
    return pl.BlockSpec((1, 1, D_MODEL), lambda b, t: (jnp.where(t == 0, nb, b), 0, k))


def _mod_kernel(c_ref, w_ref, b_ref, o_ref):
    o_ref[0] = _dot(_silu(c_ref[...]), w_ref[0], HIGHEST) + b_ref[0]


def _modulation(c_all, w_mod, b_mod):
    depth, _, n = w_mod.shape
    rows = c_all.shape[0]
    tn = 1536
    return pl.pallas_call(
        _mod_kernel,
        grid=(depth, n // tn),
        in_specs=[pl.BlockSpec((rows, D_MODEL), lambda l, j: (0, 0)),
                  pl.BlockSpec((1, D_MODEL, tn), lambda l, j: (l, 0, j)),
                  pl.BlockSpec((1, 1, tn), lambda l, j: (l, 0, j))],
        out_specs=pl.BlockSpec((1, rows, tn), lambda l, j: (l, 0, j)),
        out_shape=jax.ShapeDtypeStruct((depth, rows, n), F32),
        compiler_params=_params(2),
        name="modulation",
    )(c_all, w_mod, b_mod.reshape(depth, 1, n))


def _inproj_kernel(x_ref, sh_ref, sc_ref, w_ref, na_ref, dn_ref, z_ref, ba_ref):
    h = (x_ref[0] * (1.0 + sc_ref[0]) + sh_ref[0]).astype(BF16)
    na = _dot(h, w_ref[:, 0:3 * NA_WIDTH])
    col = lax.broadcasted_iota(jnp.int32, (1, 3 * NA_WIDTH), 1)
    na_ref[0] = (na * jnp.where(col < NA_WIDTH, NA_HEAD_DIM ** -0.5, 1.0)).astype(BF16)
    o = 3 * NA_WIDTH
    dn_ref[0] = _dot(h, w_ref[:, o:o + 3 * DN_WIDTH])
    o += 3 * DN_WIDTH
    z_ref[0] = _dot(h, w_ref[:, o:o + DN_WIDTH])
    o += DN_WIDTH
    ba_ref[0] = _dot(h, w_ref[:, o:o + LANES])


def _in_projection(xs, mods, w_in_p):
    nb, tt, _ = xs.shape
    nt = tt // TILE
    tok = lambda w: pl.BlockSpec((1, TILE, w), lambda b, t: (b, t, 0))
    return pl.pallas_call(
        _inproj_kernel,
        grid=(nb, nt),
        in_specs=[tok(D_MODEL), _mod_spec(0, nb), _mod_spec(1, nb),
                  pl.BlockSpec(w_in_p.shape, lambda b, t: (0, 0))],
        out_specs=[tok(3 * NA_WIDTH), tok(3 * DN_WIDTH), tok(DN_WIDTH), tok(LANES)],
        out_shape=[jax.ShapeDtypeStruct((nb, tt, 3 * NA_WIDTH), BF16),
                   jax.ShapeDtypeStruct((nb, tt, 3 * DN_WIDTH), F32),
                   jax.ShapeDtypeStruct((nb, tt, DN_WIDTH), F32),
                   jax.ShapeDtypeStruct((nb, tt, LANES), F32)],
        compiler_params=_params(2),
        name="in_projection",
    )(xs, mods, mods, w_in_p)


def _na_bias_table(rpb):
    qc = jnp.arange(GRID_W)[:, None]
    kc = jnp.arange(GRID_W)[None, :]
    cstart = jnp.clip(qc - NA_KC // 2, 0, GRID_W - NA_KC)
    ok = (kc >= cstart) & (kc < cstart + NA_KC)
    dc = jnp.clip(kc - qc + NA_KC - 1, 0, 2 * NA_KC - 2)
    dr = jnp.arange(NA_KR)[None, :] - jnp.arange(NA_KR)[:, None] + NA_KR - 1
    tab = rpb[:, dr[:, :, None, None], dc[None, None, :, :]]
    tab = jnp.where(ok[None, None, None], tab, NEG_BIG)
    return tab.transpose(1, 0, 3, 2, 4).reshape(NA_KR, NA_HEADS // 2, 2 * GRID_W, NA_KR * GRID_W)


def _attn_kernel(q_ref, k_ref, v_ref, bias_ref, o_ref):
    t = pl.program_id(1)
    lo = lax.broadcasted_iota(jnp.int32, (1, LANES), 1) < NA_HEAD_DIM
    hi = jnp.logical_not(lo)
    n_rows = (k_ref.shape[1] - CTX_LEN) // GRID_W
    n_keys = NA_KR * GRID_W

    def split_heads(qp):
        zero = jnp.zeros_like(qp)
        return jnp.concatenate([jnp.where(lo, qp, zero), jnp.where(hi, qp, zero)], axis=0)

    def join_heads(o, m):
        return jnp.where(lo, o[0:m], o[m:2 * m])

    @pl.when(t == 0)
    def _context():
        for pair in range(NA_HEADS // 2):
            ls = slice(LANES * pair, LANES * (pair + 1))
            s = _dot_nt(split_heads(q_ref[0, :, ls]), k_ref[0, 0:CTX_LEN, ls])
            p = jnp.exp(s - jnp.max(s, axis=-1, keepdims=True))
            o = _dot(p.astype(BF16), v_ref[0, 0:CTX_LEN, ls]) / jnp.sum(p, axis=-1, keepdims=True)
            o_ref[0, :, ls] = join_heads(o, TILE)

    @pl.when(t > 0)
    def _latent():
        for pair in range(NA_HEADS // 2):
            ls = slice(LANES * pair, LANES * (pair + 1))
            q2 = split_heads(q_ref[0, :, ls])
            s_ctx = _dot_nt(q2, k_ref[0, 0:CTX_LEN, ls])
            p_ctx, dens, o_loc = [], [], []
            for rr in range(ROWS_PER_TILE):
                r = (t - 1) * ROWS_PER_TILE + rr
                rs = jnp.clip(r - NA_KR // 2, 0, n_rows - NA_KR)
                keys = pl.ds(pl.multiple_of(CTX_LEN + rs * GRID_W, GRID_W), n_keys)
                row = lambda x: jnp.concatenate([x[GRID_W * rr:GRID_W * (rr + 1)],
                                                 x[TILE + GRID_W * rr:TILE + GRID_W * (rr + 1)]], axis=0)
                s_l = _dot_nt(row(q2), k_ref[0, keys, ls]) + bias_ref[r - rs, pair]
                s_c = row(s_ctx)
                mx = jnp.maximum(jnp.max(s_l, axis=-1, keepdims=True), jnp.max(s_c, axis=-1, keepdims=True))
                p_l = jnp.exp(s_l - mx)
                p_c = jnp.exp(s_c - mx)
                dens.append(jnp.sum(p_l, axis=-1, keepdims=True) + jnp.sum(p_c, axis=-1, keepdims=True))
                o_loc.append(_dot(p_l.astype(BF16), v_ref[0, keys, ls]))
                p_ctx.append(p_c.astype(BF16))
            o_ctx = _dot(jnp.concatenate(p_ctx, axis=0), v_ref[0, 0:CTX_LEN, ls])
            for rr in range(ROWS_PER_TILE):
                o = (o_loc[rr] + o_ctx[2 * GRID_W * rr:2 * GRID_W * (rr + 1)]) / dens[rr]
                o_ref[0, GRID_W * rr:GRID_W * (rr + 1), ls] = join_heads(o, GRID_W)


def _attention(qkv, bias_tab):
    nb, tt, _ = qkv.shape
    nt = tt // TILE
    return pl.pallas_call(
        _attn_kernel,
        grid=(nb, nt),
        in_specs=[pl.BlockSpec((1, TILE, NA_WIDTH), lambda b, t: (b, t, 0)),
                  pl.BlockSpec((1, tt, NA_WIDTH), lambda b, t: (b, 0, 1)),
                  pl.BlockSpec((1, tt, NA_WIDTH), lambda b, t: (b, 0, 2)),
                  pl.BlockSpec(bias_tab.shape, lambda b, t: (0, 0, 0, 0))],
        out_specs=pl.BlockSpec((1, TILE, NA_WIDTH), lambda b, t: (b, t, 0)),
        out_shape=jax.ShapeDtypeStruct((nb, tt, NA_WIDTH), F32),
        compiler_params=_params(2),
        name="attention",
    )(qkv, qkv, qkv, bias_tab)


def _rope_tables(seq):
    nf = DN_HEAD_DIM // 4
    inv = ROPE_THETA ** (-jnp.arange(nf, dtype=F32) / nf)
    t = jnp.arange(seq)
    lane = jnp.arange(DN_HEAD_DIM)
    pos = jnp.where(lane[None, :] < DN_HEAD_DIM // 2, (t // GRID_W)[:, None], (t % GRID_W)[:, None]).astype(F32)
    ang = pos * inv[lane % nf][None, :]
    first = (lane % (2 * nf)) < nf
    cos = jnp.cos(ang)
    sin = jnp.where(first[None, :], -jnp.sin(ang), jnp.sin(ang))
    cos = jnp.concatenate([jnp.ones((CTX_LEN, DN_HEAD_DIM), F32), cos], 0)
    sin = jnp.concatenate([jnp.zeros((CTX_LEN, DN_HEAD_DIM), F32), sin], 0)
    return cos, sin


def _dnprep_kernel(cur_ref, prev_ref, next_ref, cw_ref, cos_ref, sin_ref, q_ref, k_ref, v_ref, pad_ref):
    t = pl.program_id(1)
    nt = pl.num_programs(1)
    halo = prev_ref.shape[1]
    prev_ok = t >= 2
    next_ok = jnp.logical_and(t >= 1, t < nt - 1)
    pad_ref[0:halo] = jnp.where(prev_ok, prev_ref[0], 0.0)
    pad_ref[halo:halo + TILE] = cur_ref[0]
    pad_ref[halo + TILE:2 * halo + TILE] = jnp.where(next_ok, next_ref[0], 0.0)
    lane = lax.broadcasted_iota(jnp.int32, (1, LANES), 1)
    first = (lane % (DN_HEAD_DIM // 2)) < DN_HEAD_DIM // 4
    cos = cos_ref[...]
    sin = sin_ref[...]
    outs = (q_ref, k_ref, v_ref)
    for grp in range(3 * DN_HEADS):
        ls = slice(LANES * grp, LANES * (grp + 1))
        acc = None
        for j in range(DN_CONV):
            term = cw_ref[j:j + 1, ls] * pad_ref[pl.ds(halo - DN_CONV // 2 + j, TILE), ls]
            acc = term if acc is None else acc + term
        x = _silu(acc)
        which, head = divmod(grp, DN_HEADS)
        if which < 2:
            x = x * lax.rsqrt(jnp.sum(x * x, axis=-1, keepdims=True) + RMS_EPS)
            rot = jnp.where(first, pltpu.roll(x, LANES - DN_HEAD_DIM // 4, 1), pltpu.roll(x, DN_HEAD_DIM // 4, 1))
            x = x * cos + rot * sin
        if which == 0:
            x = x * DN_HEAD_DIM ** -0.5
        outs[which][0, :, LANES * head:LANES * (head + 1)] = x


def _dn_prepare(qkv_dn, conv_w, cos, sin):
    nb, tt, w = qkv_dn.shape
    nt = tt // TILE
    halo = 8
    hpt = TILE // halo
    cw = jnp.zeros((8, w), F32).at[:DN_CONV].set(conv_w)
    out = pl.BlockSpec((1, TILE, DN_WIDTH), lambda b, t: (b, t, 0))
    return pl.pallas_call(
        _dnprep_kernel,
        grid=(nb, nt),
        in_specs=[pl.BlockSpec((1, TILE, w), lambda b, t: (b, t, 0)),
                  pl.BlockSpec((1, halo, w), lambda b, t: (b, jnp.maximum(t * hpt - 1, 0), 0)),
                  pl.BlockSpec((1, halo, w), lambda b, t: (b, jnp.minimum((t + 1) * hpt, nt * hpt - 1), 0)),
                  pl.BlockSpec((8, w), lambda b, t: (0, 0)),
                  pl.BlockSpec((TILE, DN_HEAD_DIM), lambda b, t: (t, 0)),
                  pl.BlockSpec((TILE, DN_HEAD_DIM), lambda b, t: (t, 0))],
        out_specs=[out, out, out],
        out_shape=[jax.ShapeDtypeStruct((nb, tt, DN_WIDTH), F32)] * 3,
        scratch_shapes=[pltpu.VMEM((TILE + 2 * halo, w), F32)],
        compiler_params=_params(2),
        name="dn_prepare",
    )(qkv_dn, qkv_dn, qkv_dn, cw, cos, sin)


def _gate_kernel(a_ref, b_ref, alog_ref, dtb_ref, g_ref, beta_ref):
    g = -jnp.exp(alog_ref[...]) * _softplus(a_ref[0] + dtb_ref[...])
    beta_ref[0] = jax.nn.sigmoid(b_ref[0])
    ii = lax.broadcasted_iota(jnp.int32, (CHUNK, CHUNK), 0)
    jj = lax.broadcasted_iota(jnp.int32, (CHUNK, CHUNK), 1)
    g_ref[0, 0] = _dot(g[0], (ii <= jj).astype(F32), HIGHEST)
    g_ref[0, 1] = _dot(g[1], (ii >= jj).astype(F32), HIGHEST)


def _gates(ba, a_log, dt_bias):
    nb, tt, _ = ba.shape
    nc = tt // CHUNK
    nh = DN_HEADS
    rows = lambda x: x.transpose(0, 2, 1).reshape(nb, 2, nh * nc, CHUNK)
    spec = pl.BlockSpec((1, 2, nh * nc, CHUNK), lambda b: (b, 0, 0, 0))
    par = lambda p: jnp.broadcast_to(p.reshape(2, nh, 1, 1).astype(F32), (2, nh, nc, CHUNK)).reshape(2, nh * nc, CHUNK)
    pspec = pl.BlockSpec((2, nh * nc, CHUNK), lambda b: (0, 0, 0))
    gcum, beta = pl.pallas_call(
        _gate_kernel,
        grid=(nb,),
        in_specs=[spec, spec, pspec, pspec],
        out_specs=[spec, spec],
        out_shape=[jax.ShapeDtypeStruct((nb, 2, nh * nc, CHUNK), F32)] * 2,
        compiler_params=_params(1),
        name="dn_gates",
    )(rows(ba[..., 2 * nh:4 * nh]), rows(ba[..., 0:2 * nh]), par(a_log), par(dt_bias))
    return gcum.reshape(nb, 2 * nh, nc, CHUNK), beta.reshape(nb, 2 * nh, nc, CHUNK)


def _gdn_masks(d):
    width = DN_HEADS * CHUNK
    lane = lax.broadcasted_iota(jnp.int32, (CHUNK, width), 1)
    ri = lax.broadcasted_iota(jnp.int32, (CHUNK, width), 0)
    jj = lane % CHUNK
    lower = (ri >= jj) if d == 0 else (ri <= jj)
    strict = (ri > jj) if d == 0 else (ri < jj)
    return dict(hb=lane // CHUNK, lower=lower, strict=strict, eye=(ri == jj).astype(F32))


def _blockdiag(xp, hb):
    return jnp.concatenate([jnp.where(hb == h, xp, 0.0) for h in range(DN_HEADS)], axis=0).astype(BF16)


def _gdn_setup(q, k, v, col, row, d, mk):
    nh = DN_HEADS
    hb = mk["hb"]
    stack = lambda x: jnp.concatenate([x[:, LANES * h:LANES * (h + 1)] for h in range(nh)], axis=0)
    colb = lambda c, w: jnp.broadcast_to(col[:, c:c + 1], (CHUNK, w))
    ks, qs, vs = stack(k), stack(q), stack(v)
    gs = jnp.concatenate([colb(nh * d + h, LANES) for h in range(nh)], axis=0)
    bs = jnp.concatenate([colb(2 * nh + nh * d + h, LANES) for h in range(nh)], axis=0)
    eg = jnp.exp(gs)
    kbs = ks * bs
    rhs = jnp.concatenate([vs * bs, kbs * eg], axis=1).astype(BF16)
    full = _dot_nt(jnp.concatenate([kbs, qs], axis=0).astype(BF16), ks.astype(BF16))

    def pack(lo):
        acc = full[lo:lo + CHUNK]
        for h in range(1, nh):
            acc = jnp.where(hb == h, full[lo + CHUNK * h:lo + CHUNK * (h + 1)], acc)
        return acc

    cp = colb(nh * d, nh * CHUNK)
    for h in range(1, nh):
        cp = jnp.where(hb == h, colb(nh * d + h, nh * CHUNK), cp)
    decay = jnp.exp(jnp.where(mk["lower"], cp - row, NEG_BIG))
    m_p = -jnp.where(mk["strict"], pack(0) * decay, 0.0)
    last = CHUNK - 1 if d == 0 else 0
    g_last = [gs[CHUNK * h + last:CHUNK * h + last + 1] for h in range(nh)]
    return dict(m=m_p, p=mk["eye"] + m_p, rhs=rhs, qg=qs * eg, k=ks, g=gs, g_last=g_last,
                intra=_blockdiag(pack(nh * CHUNK) * decay, hb))


def _gdn_kernel(qf, kf, vf, colf, rowf, qb, kb, vb, colb, rowb, of_ref, ob_ref, s_ref):
    @pl.when(pl.program_id(1) == 0)
    def _reset():
        s_ref[...] = jnp.zeros_like(s_ref)

    nh = DN_HEADS
    srcs = ((qf, kf, vf, colf, rowf, of_ref), (qb, kb, vb, colb, rowb, ob_ref))
    masks = (_gdn_masks(0), _gdn_masks(1))
    jobs = [(d, step if d == 0 else CHUNKS_PER_TILE - 1 - step)
            for step in range(CHUNKS_PER_TILE) for d in range(2)]
    par = []
    for d, ch in jobs:
        q, k, v, col, row, _ = srcs[d]
        rows = slice(CHUNK * ch, CHUNK * (ch + 1))
        par.append(_gdn_setup(q[0, rows, :], k[0, rows, :], v[0, rows, :], col[0, rows, :],
                              row[0, ch, d:d + 1, :], d, masks[d]))

    span = 2
    while span < CHUNK:
        for (d, _), p in zip(jobs, par):
            p["m"] = _dot(p["m"].astype(BF16), _blockdiag(p["m"], masks[d]["hb"]))
        for (d, _), p in zip(jobs, par):
            p["p"] = p["p"] + _dot(p["p"].astype(BF16), _blockdiag(p["m"], masks[d]["hb"]))
        span *= 2
    for (d, _), p in zip(jobs, par):
        uw = _dot(_blockdiag(p["p"], masks[d]["hb"]), p["rhs"])
        p["u"], p["w"] = uw[:, 0:LANES], uw[:, LANES:2 * LANES]

    hs = lambda h: slice(CHUNK * h, CHUNK * (h + 1))
    for step in range(CHUNKS_PER_TILE):
        group = [(d, ch, p) for (d, ch), p in zip(jobs, par)][2 * step:2 * step + 2]
        chains = [(d, p, h) for d, _, p in group for h in range(nh)]
        states = [s_ref[nh * d + h] for d, _, h in chains]
        res = [_dot(jnp.concatenate([p["w"][hs(h)], p["qg"][hs(h)]], axis=0).astype(BF16), s.astype(BF16))
               for (_, p, h), s in zip(chains, states)]
        vnew = [p["u"][hs(h)] - r[0:CHUNK] for (_, p, h), r in zip(chains, res)]
        for (d, p, h), s, vn in zip(chains, states, vnew):
            g_last = p["g_last"][h]
            kg = (p["k"][hs(h)] * jnp.exp(g_last - p["g"][hs(h)])).astype(BF16)
            s_ref[nh * d + h] = s * jnp.exp(g_last) + _dot_tn(kg, vn.astype(BF16))
        for gi, (d, ch, p) in enumerate(group):
            o_intra = _dot(p["intra"], jnp.concatenate(vnew[nh * gi:nh * (gi + 1)], axis=0).astype(BF16))
            for h in range(nh):
                srcs[d][5][0, CHUNK * ch:CHUNK * (ch + 1), LANES * h:LANES * (h + 1)] = (
                    res[nh * gi + h][CHUNK:2 * CHUNK] + o_intra[hs(h)])


def _gdn(q, k, v, gcum, beta):
    nb, tt, _ = q.shape
    nt = tt // TILE
    nh = DN_HEADS
    col = jnp.concatenate([gcum, beta], axis=1).transpose(0, 2, 3, 1).reshape(nb, tt, 4 * nh)
    row = gcum.reshape(nb, 2, nh, tt // CHUNK, CHUNK).transpose(0, 3, 1, 2, 4).reshape(nb, tt // CHUNK, 2, nh * CHUNK)
    fwd = lambda b, t: (b, t, 0)
    bwd = lambda b, t: (b, jnp.where(t == 0, 0, nt - t), 0)
    tok = lambda w, im: pl.BlockSpec((1, TILE, w), im)
    rowspec = lambda im: pl.BlockSpec((1, CHUNKS_PER_TILE, 2, nh * CHUNK), lambda b, t: im(b, t) + (0,))
    side = lambda im: [tok(DN_WIDTH, im), tok(DN_WIDTH, im), tok(DN_WIDTH, im), tok(4 * nh, im), rowspec(im)]
    return pl.pallas_call(
        _gdn_kernel,
        grid=(nb, nt),
        in_specs=side(fwd) + side(bwd),
        out_specs=[tok(DN_WIDTH, fwd), tok(DN_WIDTH, bwd)],
        out_shape=[jax.ShapeDtypeStruct((nb, tt, DN_WIDTH), F32)] * 2,
        scratch_shapes=[pltpu.VMEM((2 * nh, DN_HEAD_DIM, DN_HEAD_DIM), F32)],
        compiler_params=_params(2),
        name="gated_deltanet",
    )(q, k, v, col, row, q, k, v, col, row)


def _layernorm(x, g, b):
    mu = jnp.mean(x, axis=-1, keepdims=True)
    xc = x - mu
    var = jnp.mean(xc * xc, axis=-1, keepdims=True)
    return xc * lax.rsqrt(var + LN_EPS) * g + b


def _merge_kernel(alpha, with_router, ona_ref, of_ref, ob_ref, z_ref, x_ref, g1_ref, sh2_ref, sc2_ref,
                  nag_ref, dng_ref, wout_ref, lng_ref, lnb_ref, *rest):
    if with_router:
        router_ref, x1_ref, h2_ref, route_ref, count_ref, base_ref = rest
    else:
        x1_ref, h2_ref = rest
    ona = ona_ref[0]
    na = ona * lax.rsqrt(jnp.mean(ona * ona, axis=-1, keepdims=True) + RMS_EPS) * nag_ref[...]
    y = _dot(na.astype(BF16), wout_ref[0:NA_WIDTH, :])
    for h in range(DN_HEADS):
        ls = slice(LANES * h, LANES * (h + 1))
        od = of_ref[0, :, ls] + ob_ref[0, :, ls]
        dn = od * lax.rsqrt(jnp.mean(od * od, axis=-1, keepdims=True) + RMS_EPS) * dng_ref[...]
        dn = dn * _silu(z_ref[0, :, ls])
        y = y + _dot(dn.astype(BF16), wout_ref[NA_WIDTH + LANES * h:NA_WIDTH + LANES * (h + 1), :])
    x1 = _layernorm(alpha * x_ref[0] + g1_ref[0] * y, lng_ref[...], lnb_ref[...])
    x1_ref[0] = x1
    h2 = x1 * (1.0 + sc2_ref[0]) + sh2_ref[0]
    h2_ref[0] = h2.astype(h2_ref.dtype)
    if not with_router:
        return

    @pl.when(jnp.logical_and(pl.program_id(0) == 0, pl.program_id(1) == 0))
    def _reset():
        base_ref[...] = jnp.zeros_like(base_ref)

    logits = _dot(h2, router_ref[...], HIGHEST)
    lane = lax.broadcasted_iota(jnp.int32, logits.shape, 1).astype(F32)
    lg = jnp.where(lane < N_EXPERTS, logits, NEG_BIG)
    m0 = jnp.max(lg, axis=-1, keepdims=True)
    e0 = jnp.min(jnp.where(lg == m0, lane, float(LANES)), axis=-1, keepdims=True)
    lg1 = jnp.where(lane == e0, NEG_BIG, lg)
    m1 = jnp.max(lg1, axis=-1, keepdims=True)
    e1 = jnp.min(jnp.where(lg1 == m1, lane, float(LANES)), axis=-1, keepdims=True)
    t1 = jnp.exp(m1 - m0)
    w0 = 1.0 / (1.0 + t1)
    w1 = t1 / (1.0 + t1)
    hot0 = lane == e0
    hot1 = lane == e1
    onehot = jnp.logical_or(hot0, hot1).astype(BF16)
    n = onehot.shape[0]
    earlier = (lax.broadcasted_iota(jnp.int32, (n, n), 0) > lax.broadcasted_iota(jnp.int32, (n, n), 1)).astype(BF16)
    before = base_ref[...] + _dot(earlier, onehot)
    rank0 = jnp.sum(jnp.where(hot0, before, 0.0), axis=-1, keepdims=True)
    rank1 = jnp.sum(jnp.where(hot1, before, 0.0), axis=-1, keepdims=True)
    total = base_ref[...] + jnp.sum(onehot.astype(F32), axis=0, keepdims=True)
    base_ref[...] = total
    count_ref[...] = jnp.broadcast_to(total, count_ref.shape)
    route = jnp.zeros_like(logits)
    for k, val in enumerate((e0, e1, w0, w1, rank0, rank1)):
        route = jnp.where(lane == k, val, route)
    route_ref[0] = route


def _merge(alpha, o_na, o_f, o_b, z, xs, mods, na_g, dn_g, w_out, ln_g, ln_b, router_p):
    nb, tt, _ = xs.shape
    nt = tt // TILE
    tok = lambda w: pl.BlockSpec((1, TILE, w), lambda b, t: (b, t, 0))
    vec = lambda w: pl.BlockSpec((1, w), lambda b, t: (0, 0))
    with_router = router_p is not None
    in_specs = [tok(NA_WIDTH), tok(DN_WIDTH), tok(DN_WIDTH), tok(DN_WIDTH), tok(D_MODEL),
                _mod_spec(2, nb), _mod_spec(3, nb), _mod_spec(4, nb), vec(NA_WIDTH), vec(DN_HEAD_DIM),
                pl.BlockSpec(w_out.shape, lambda b, t: (0, 0)), vec(D_MODEL), vec(D_MODEL)]
    args = [o_na, o_f, o_b, z, xs, mods, mods, mods, na_g.reshape(1, -1), dn_g.reshape(1, -1), w_out,
            ln_g.reshape(1, -1), ln_b.reshape(1, -1)]
    out_specs = [tok(D_MODEL), tok(D_MODEL)]
    out_shape = [jax.ShapeDtypeStruct((nb, tt, D_MODEL), F32),
                 jax.ShapeDtypeStruct((nb, tt, D_MODEL), F32 if with_router else BF16)]
    scratch = []
    if with_router:
        in_specs.append(pl.BlockSpec(router_p.shape, lambda b, t: (0, 0)))
        args.append(router_p)
        out_specs += [tok(LANES), pl.BlockSpec((8, LANES), lambda b, t: (0, 0))]
        out_shape += [jax.ShapeDtypeStruct((nb, tt, LANES), F32), jax.ShapeDtypeStruct((8, LANES), F32)]
        scratch = [pltpu.VMEM((1, LANES), F32)]
    return pl.pallas_call(
        functools.partial(_merge_kernel, alpha, with_router),
        grid=(nb, nt),
        in_specs=in_specs,
        out_specs=out_specs,
        out_shape=out_shape,
        scratch_shapes=scratch,
        compiler_params=_params(2),
        name="merge_router" if with_router else "merge",
    )(*args)


def _resid_ln_kernel(alpha, x_ref, f_ref, g2_ref, lng_ref, lnb_ref, o_ref):
    o_ref[0] = _layernorm(alpha * x_ref[0] + g2_ref[0] * f_ref[0], lng_ref[...], lnb_ref[...])


def _resid_ln(alpha, xs, f, mods, ln_g, ln_b):
    nb, tt, _ = xs.shape
    tok = pl.BlockSpec((1, TILE, D_MODEL), lambda b, t: (b, t, 0))
    vec = pl.BlockSpec((1, D_MODEL), lambda b, t: (0, 0))
    return pl.pallas_call(
        functools.partial(_resid_ln_kernel, alpha),
        grid=(nb, tt // TILE),
        in_specs=[tok, tok, _mod_spec(5, nb), vec, vec],
        out_specs=tok,
        out_shape=jax.ShapeDtypeStruct(xs.shape, F32),
        compiler_params=_params(2),
        name="residual_layernorm",
    )(xs, f, mods, ln_g.reshape(1, -1), ln_b.reshape(1, -1))


def _ffn_kernel(te_ref, nact_ref, x_ref, wg_ref, wu_ref, wd_ref, o_ref, act_ref):
    del te_ref
    f = pl.program_id(1)
    active = pl.program_id(0) < nact_ref[0]

    @pl.when(jnp.logical_not(active))
    def _unused_tile():
        o_ref[...] = jnp.zeros_like(o_ref)

    @pl.when(active)
    def _active():
        x = x_ref[...].astype(BF16)
        tf = act_ref.shape[1]
        off = 0
        while off < tf:
            n = min(FFN_SUB, tf - off)
            gt = _dot(x, wg_ref[0, :, off:off + n])
            up = _dot(x, wu_ref[0, :, off:off + n])
            act_ref[:, off:off + n] = (_silu(gt) * up).astype(BF16)
            off += n
        part = _dot(act_ref[...], wd_ref[0])

        @pl.when(f == 0)
        def _first():
            o_ref[...] = part

        @pl.when(f > 0)
        def _rest():
            o_ref[...] += part


def _ffn(x, tile_expert, n_active, wg, wu, wd, tm, tf):
    s, dm = x.shape
    ff = wg.shape[2]
    nf = ff // tf
    fblk = lambda i, f, na: jnp.where(i < na[0], f, nf - 1)
    grid_spec = pltpu.PrefetchScalarGridSpec(
        num_scalar_prefetch=2,
        grid=(s // tm, nf),
        in_specs=[pl.BlockSpec((tm, dm), lambda i, f, te, na: (i, 0)),
                  pl.BlockSpec((1, dm, tf), lambda i, f, te, na: (te[i], 0, fblk(i, f, na))),
                  pl.BlockSpec((1, dm, tf), lambda i, f, te, na: (te[i], 0, fblk(i, f, na))),
                  pl.BlockSpec((1, tf, dm), lambda i, f, te, na: (te[i], fblk(i, f, na), 0))],
        out_specs=pl.BlockSpec((tm, dm), lambda i, f, te, na: (i, 0)),
        scratch_shapes=[pltpu.VMEM((tm, tf), BF16)],
    )
    return pl.pallas_call(
        _ffn_kernel,
        grid_spec=grid_spec,
        out_shape=jax.ShapeDtypeStruct((s, dm), F32),
        compiler_params=_params(2),
        name="swiglu",
    )(tile_expert, n_active, x, wg, wu, wd)


def _dense_ffn(h2, wg, wu, wd):
    nb, tt, dm = h2.shape
    s = nb * tt
    tm = FFN_TM if s % FFN_TM == 0 else TILE
    ff = wg.shape[1]
    tf = ff // 2 if (ff // 2) % LANES == 0 else ff
    n_tiles = s // tm
    out = _ffn(h2.reshape(s, dm), jnp.zeros((n_tiles,), jnp.int32), jnp.full((1,), n_tiles, jnp.int32),
               wg[None], wu[None], wd[None], tm, tf)
    return out.reshape(nb, tt, dm)


def _row_copy(src_ref, src_row, dst_ref, dst_row, sem):
    return pltpu.make_async_copy(src_ref.at[pl.ds(src_row, 1)], dst_ref.at[pl.ds(dst_row, 1)], sem)


def _dispatch_kernel(tm, meta_ref, slot_ref, h_ref, xs_ref, zero_ref, sem):
    n_tok = h_ref.shape[0]
    n_tiles = xs_ref.shape[0] // tm

    @pl.when(pl.program_id(0) == 0)
    def _zero_unfilled():
        zero_ref[...] = jnp.zeros_like(zero_ref)

        def zero_tile(tile):
            cp = pltpu.make_async_copy(zero_ref, xs_ref.at[pl.ds(pl.multiple_of(tile * tm, tm), tm)], sem)
            cp.start()
            cp.wait()

        for e in range(N_EXPERTS):
            cnt = meta_ref[e]

            @pl.when(cnt % tm != 0)
            def _():
                zero_tile(meta_ref[N_EXPERTS + e] + cnt // tm)

        def tail(tile, carry):
            zero_tile(tile)
            return carry
        lax.fori_loop(meta_ref[2 * N_EXPERTS], n_tiles, tail, 0)

    def copies(i):
        return [_row_copy(h_ref, i, xs_ref, slot_ref[TOP_K * i + k], sem) for k in range(TOP_K)]

    def issue(i, carry):
        for cp in copies(i):
            cp.start()
        return carry

    def drain(i, carry):
        for cp in copies(i):
            cp.wait()
        return carry

    lax.fori_loop(0, n_tok, issue, 0)
    lax.fori_loop(0, n_tok, drain, 0)


def _dispatch(h2, slots, meta, tm, n_tiles):
    s, dm = h2.shape
    grid_spec = pltpu.PrefetchScalarGridSpec(
        num_scalar_prefetch=1,
        grid=(s // TILE,),
        in_specs=[pl.BlockSpec((TOP_K * TILE,), lambda i, meta: (i,), memory_space=pltpu.SMEM),
                  pl.BlockSpec((TILE, dm), lambda i, meta: (i, 0))],
        out_specs=pl.BlockSpec(memory_space=pl.ANY),
        scratch_shapes=[pltpu.VMEM((tm, dm), F32), pltpu.SemaphoreType.DMA(())],
    )
    return pl.pallas_call(
        functools.partial(_dispatch_kernel, tm),
        grid_spec=grid_spec,
        out_shape=jax.ShapeDtypeStruct((n_tiles * tm, dm), F32),
        compiler_params=_params(1),
        name="moe_dispatch",
    )(meta, slots, h2)


def _combine_kernel(alpha, slot_ref, x_ref, route_ref, g2_ref, lng_ref, lnb_ref, y_ref, o_ref, buf_ref, sem):
    n_tok = x_ref.shape[1]

    def copies(i):
        return [_row_copy(y_ref, slot_ref[TOP_K * i + k], buf_ref.at[k], i, sem) for k in range(TOP_K)]

    def issue(i, carry):
        for cp in copies(i):
            cp.start()
        return carry

    def drain(i, carry):
        for cp in copies(i):
            cp.wait()
        return carry

    lax.fori_loop(0, n_tok, issue, 0)
    lax.fori_loop(0, n_tok, drain, 0)
    route = route_ref[0]
    f = route[:, 2:3] * buf_ref[0] + route[:, 3:4] * buf_ref[1]
    o_ref[0] = _layernorm(alpha * x_ref[0] + g2_ref[0] * f, lng_ref[...], lnb_ref[...])


def _combine_ln(alpha, x1, route, slots, y, mods, ln_g, ln_b):
    nb, tt, dm = x1.shape
    nt = tt // TILE
    tok = lambda w: pl.BlockSpec((1, TILE, w), lambda b, t: (b, t, 0))
    vec = pl.BlockSpec((1, dm), lambda b, t: (0, 0))
    return pl.pallas_call(
        functools.partial(_combine_kernel, alpha),
        grid=(nb, nt),
        in_specs=[pl.BlockSpec((TOP_K * TILE,), lambda b, t: (b * nt + t,), memory_space=pltpu.SMEM),
                  tok(dm), tok(LANES), _mod_spec(5, nb), vec, vec,
                  pl.BlockSpec(memory_space=pl.ANY)],
        out_specs=tok(dm),
        out_shape=jax.ShapeDtypeStruct(x1.shape, F32),
        scratch_shapes=[pltpu.VMEM((TOP_K, TILE, dm), F32), pltpu.SemaphoreType.DMA(())],
        compiler_params=_params(2),
        name="moe_combine_layernorm",
    )(slots, x1, route, mods, ln_g.reshape(1, -1), ln_b.reshape(1, -1), y)


def _moe_layer(alpha, x1, h2, route, counts, mods, wg, wu, wd, ln_g, ln_b):
    nb, tt, dm = x1.shape
    s = nb * tt
    tm = FFN_TM
    ne = N_EXPERTS
    cnt = counts[0, :ne].astype(jnp.int32)
    tiles = (cnt + tm - 1) // tm
    ends = jnp.cumsum(tiles)
    first = ends - tiles
    n_active = ends[ne - 1:ne]
    n_tiles = (TOP_K * s + ne * (tm - 1)) // tm
    r = route.reshape(s, LANES)
    chosen = r[:, 0:TOP_K].astype(jnp.int32)[..., None] == jnp.arange(ne)
    slots = (jnp.sum(jnp.where(chosen, first * tm, 0), axis=-1) + r[:, 4:4 + TOP_K].astype(jnp.int32)).reshape(-1)
    i = jnp.minimum(jnp.arange(n_tiles), n_active - 1)
    te = jnp.sum(i[:, None] >= ends[None, :], axis=1).astype(jnp.int32)
    xs = _dispatch(h2.reshape(s, dm), slots, jnp.concatenate([cnt, first, n_active]), tm, n_tiles)
    y = _ffn(xs, te, n_active, wg, wu, wd, tm, wg.shape[2] // 2)
    return _combine_ln(alpha, x1, route, slots, y, mods, ln_g, ln_b)


def kernel(x, c, ctx, c_ctx, w_mod, b_mod, w_in, dn_conv_w, dn_a_log, dn_dt_bias, dn_norm_g, na_rpb, na_out_g, w_out, ln1_g, ln1_b, ln2_g, ln2_b, ffn_w_gate, ffn_w_up, ffn_w_down, moe_router, moe_w_gate, moe_w_up, moe_w_down):
    nb, seq, dm = x.shape
    depth = w_mod.shape[0]
    alpha = (2.0 * depth) ** 0.25
    n_ctx = ctx.shape[1]
    assert n_ctx == CTX_LEN == TILE and seq % TILE == 0 and dm == D_MODEL

    xs = jnp.concatenate([ctx, x], axis=1)
    rows = -(-(nb + 1) // 8) * 8
    c_all = jnp.zeros((rows, dm), F32).at[:nb].set(c).at[nb].set(c_ctx)
    mods_all = _modulation(c_all, w_mod, b_mod)
    cos, sin = _rope_tables(seq)
    p_in = w_in.shape[2]
    p_pad = 3 * NA_WIDTH + 4 * DN_WIDTH + LANES

    for l in range(depth):
        mods = mods_all[l].reshape(rows, 1, 6 * dm)
        w_in_p = jnp.zeros((dm, p_pad), BF16).at[:, :p_in].set(w_in[l].astype(BF16))
        qkv_na, qkv_dn, z, ba = _in_projection(xs, mods, w_in_p)
        o_na = _attention(qkv_na, _na_bias_table(na_rpb[l]))
        q_dn, k_dn, v_dn = _dn_prepare(qkv_dn, dn_conv_w[l], cos, sin)
        gcum, beta = _gates(ba, dn_a_log[l], dn_dt_bias[l])
        o_f, o_b = _gdn(q_dn, k_dn, v_dn, gcum, beta)
        moe = l % 2 == 1
        i = l // 2
        router_p = None
        if moe:
            router_p = jnp.zeros((dm, LANES), F32).at[:, :N_EXPERTS].set(moe_router[i])
        merged = _merge(alpha, o_na, o_f, o_b, z, xs, mods, na_out_g[l], dn_norm_g[l], w_out[l].astype(BF16),
                        ln1_g[l], ln1_b[l], router_p)
        if moe:
            x1, h2, route, counts = merged
            xs = _moe_layer(alpha, x1, h2, route, counts, mods, moe_w_gate[i].astype(BF16),
                            moe_w_up[i].astype(BF16), moe_w_down[i].astype(BF16), ln2_g[l], ln2_b[l])
        else:
            x1, h2 = merged
            f = _dense_ffn(h2, ffn_w_gate[i].astype(BF16), ffn_w_up[i].astype(BF16), ffn_w_down[i].astype(BF16))
            xs = _resid_ln(alpha, x1, f, mods, ln2_g[l], ln2_b[l])
    return xs[:, n_ctx:]
```

```python
import functools

import jax
import jax.numpy as jnp
from jax import lax
from jax.experimental import pallas as pl
from jax.experimental.pallas import tpu as pltpu

F32 = jnp.float32
BF16 = jnp.bfloat16
HIGHEST = lax.Precision.HIGHEST

D_MODEL = 1024
GRID_W = 64
CTX_LEN = 256
NA_HEADS = 8
NA_HEAD_DIM = 64
NA_WIDTH = NA_HEADS * NA_HEAD_DIM
NA_KR = 8
NA_KC = 16
DN_HEAD_DIM = 128
DN_HEADS = 4
DN_WIDTH = DN_HEADS * DN_HEAD_DIM
DN_CONV = 5
CHUNK = 64
ROPE_THETA = 10000.0
N_EXPERTS = 8
TOP_K = 2
LN_EPS = 1e-5
RMS_EPS = 1e-6
NEG_BIG = -1e30

TILE = 256
CHUNKS_PER_TILE = TILE // CHUNK
ROWS_PER_TILE = TILE // GRID_W
LANES = 128
VMEM_LIMIT = 56 * 1024 * 1024
FFN_TM = 1024
FFN_SUB = 256


def _params(n_axes, vmem=VMEM_LIMIT):
    return pltpu.CompilerParams(dimension_semantics=("arbitrary",) * n_axes, vmem_limit_bytes=vmem)


def _dot(a, b, precision=None):
    return jnp.dot(a, b, preferred_element_type=F32, precision=precision)


def _dot_nt(a, b):
    return lax.dot_general(a, b, (((1,), (1,)), ((), ())), preferred_element_type=F32)


def _dot_tn(a, b):
    return lax.dot_general(a, b, (((0,), (0,)), ((), ())), preferred_element_type=F32)


def _silu(x):
    return x * jax.nn.sigmoid(x)


def _softplus(x):
    return jnp.maximum(x, 0.0) + jnp.log1p(jnp.exp(-jnp.abs(x)))


def _mod_spec(k, nb):
    return pl.BlockSpec((1, 1, D_MODEL), lambda b, t: (jnp.where(t == 0, nb, b), 0, k))


def _mod_kernel(c_ref, w_ref, b_ref, o_ref):
    o_ref[0] = _dot(_silu(c_ref[...]), w_ref[0], HIGHEST) + b_ref[0]


def _modulation(c_all, w_mod, b_mod):
    depth, _, n = w_mod.shape
    rows = c_all.shape[0]
    tn = 1536
    return pl.pallas_call(
        _mod_kernel,
        grid=(depth, n // tn),
        in_specs=[pl.BlockSpec((rows, D_MODEL), lambda l, j: (0, 0)),
                  pl.BlockSpec((1, D_MODEL, tn), lambda l, j: (l, 0, j)),
                  pl.BlockSpec((1, 1, tn), lambda l, j: (l, 0, j))],
        out_specs=pl.BlockSpec((1, rows, tn), lambda l, j: (l, 0, j)),
        out_shape=jax.ShapeDtypeStruct((depth, rows, n), F32),
        compiler_params=_params(2),
        name="modulation",
    )(c_all, w_mod, b_mod.reshape(depth, 1, n))


def _inproj_kernel(x_ref, sh_ref, sc_ref, w_ref, na_ref, dn_ref, z_ref, ba_ref):
    h = (x_ref[0] * (1.0 + sc_ref[0]) + sh_ref[0]).astype(BF16)
    na = _dot(h, w_ref[:, 0:3 * NA_WIDTH])
    col = lax.broadcasted_iota(jnp.int32, (1, 3 * NA_WIDTH), 1)
    na_ref[0] = (na * jnp.where(col < NA_WIDTH, NA_HEAD_DIM ** -0.5, 1.0)).astype(BF16)
    o = 3 * NA_WIDTH
    dn_ref[0] = _dot(h, w_ref[:, o:o + 3 * DN_WIDTH]).astype(dn_ref.dtype)
    o += 3 * DN_WIDTH
    z_ref[0] = _dot(h, w_ref[:, o:o + DN_WIDTH]).astype(z_ref.dtype)
    o += DN_WIDTH
    ba_ref[0] = _dot(h, w_ref[:, o:o + LANES])


def _in_projection(xs, mods, w_in_p):
    nb, tt, _ = xs.shape
    nt = tt // TILE
    tok = lambda w: pl.BlockSpec((1, TILE, w), lambda b, t: (b, t, 0))
    return pl.pallas_call(
        _inproj_kernel,
        grid=(nb, nt),
        in_specs=[tok(D_MODEL), _mod_spec(0, nb), _mod_spec(1, nb),
                  pl.BlockSpec(w_in_p.shape, lambda b, t: (0, 0))],
        out_specs=[tok(3 * NA_WIDTH), tok(3 * DN_WIDTH), tok(DN_WIDTH), tok(LANES)],
        out_shape=[jax.ShapeDtypeStruct((nb, tt, 3 * NA_WIDTH), BF16),
                   jax.ShapeDtypeStruct((nb, tt, 3 * DN_WIDTH), BF16),
                   jax.ShapeDtypeStruct((nb, tt, DN_WIDTH), BF16),
                   jax.ShapeDtypeStruct((nb, tt, LANES), F32)],
        compiler_params=_params(2),
        name="in_projection",
    )(xs, mods, mods, w_in_p)


def _na_bias_table(rpb):
    qc = jnp.arange(GRID_W)[:, None]
    kc = jnp.arange(GRID_W)[None, :]
    cstart = jnp.clip(qc - NA_KC // 2, 0, GRID_W - NA_KC)
    ok = (kc >= cstart) & (kc < cstart + NA_KC)
    left = GRID_W - NA_KC
    rp = jnp.pad(rpb, ((0, 0), (0, 0), (left, left)))
    toe = jnp.stack([rp[:, :, GRID_W - 1 - q:2 * GRID_W - 1 - q] for q in range(GRID_W)], axis=2)
    toe = jnp.where(ok, toe, NEG_BIG)
    tabs = [toe[:, NA_KR - 1 - v:2 * NA_KR - 1 - v].transpose(0, 2, 1, 3).reshape(NA_HEADS, GRID_W, NA_KR * GRID_W)
            for v in range(NA_KR)]
    return jnp.stack(tabs).reshape(NA_KR, NA_HEADS // 2, 2 * GRID_W, NA_KR * GRID_W)


def _attn_kernel(q_ref, k_ref, v_ref, bias_ref, o_ref):
    t = pl.program_id(1)
    lo = lax.broadcasted_iota(jnp.int32, (1, LANES), 1) < NA_HEAD_DIM
    hi = jnp.logical_not(lo)
    n_rows = (k_ref.shape[1] - CTX_LEN) // GRID_W
    n_keys = NA_KR * GRID_W

    def split_heads(qp):
        zero = jnp.zeros_like(qp)
        return jnp.concatenate([jnp.where(lo, qp, zero), jnp.where(hi, qp, zero)], axis=0)

    def join_heads(o, m):
        return jnp.where(lo, o[0:m], o[m:2 * m])

    @pl.when(t == 0)
    def _context():
        for pair in range(NA_HEADS // 2):
            ls = slice(LANES * pair, LANES * (pair + 1))
            s = _dot_nt(split_heads(q_ref[0, :, ls]), k_ref[0, 0:CTX_LEN, ls])
            p = jnp.exp(s - jnp.max(s, axis=-1, keepdims=True))
            o = _dot(p.astype(BF16), v_ref[0, 0:CTX_LEN, ls]) / jnp.sum(p, axis=-1, keepdims=True)
            o_ref[0, :, ls] = join_heads(o, TILE).astype(o_ref.dtype)

    @pl.when(t > 0)
    def _latent():
        for pair in range(NA_HEADS // 2):
            ls = slice(LANES * pair, LANES * (pair + 1))
            q2 = split_heads(q_ref[0, :, ls])
            s_ctx = _dot_nt(q2, k_ref[0, 0:CTX_LEN, ls])
            p_ctx, dens, o_loc = [], [], []
            for rr in range(ROWS_PER_TILE):
                r = (t - 1) * ROWS_PER_TILE + rr
                rs = jnp.clip(r - NA_KR // 2, 0, n_rows - NA_KR)
                keys = pl.ds(pl.multiple_of(CTX_LEN + rs * GRID_W, GRID_W), n_keys)
                row = lambda x: jnp.concatenate([x[GRID_W * rr:GRID_W * (rr + 1)],
                                                 x[TILE + GRID_W * rr:TILE + GRID_W * (rr + 1)]], axis=0)
                s_l = _dot_nt(row(q2), k_ref[0, keys, ls]) + bias_ref[r - rs, pair]
                s_c = row(s_ctx)
                mx = jnp.maximum(jnp.max(s_l, axis=-1, keepdims=True), jnp.max(s_c, axis=-1, keepdims=True))
                p_l = jnp.exp(s_l - mx)
                p_c = jnp.exp(s_c - mx)
                dens.append(jnp.sum(p_l, axis=-1, keepdims=True) + jnp.sum(p_c, axis=-1, keepdims=True))
                o_loc.append(_dot(p_l.astype(BF16), v_ref[0, keys, ls]))
                p_ctx.append(p_c.astype(BF16))
            o_ctx = _dot(jnp.concatenate(p_ctx, axis=0), v_ref[0, 0:CTX_LEN, ls])
            for rr in range(ROWS_PER_TILE):
                o = (o_loc[rr] + o_ctx[2 * GRID_W * rr:2 * GRID_W * (rr + 1)]) / dens[rr]
                o_ref[0, GRID_W * rr:GRID_W * (rr + 1), ls] = join_heads(o, GRID_W).astype(o_ref.dtype)


def _attention(qkv, bias_tab):
    nb, tt, _ = qkv.shape
    nt = tt // TILE
    return pl.pallas_call(
        _attn_kernel,
        grid=(nb, nt),
        in_specs=[pl.BlockSpec((1, TILE, NA_WIDTH), lambda b, t: (b, t, 0)),
                  pl.BlockSpec((1, tt, NA_WIDTH), lambda b, t: (b, 0, 1)),
                  pl.BlockSpec((1, tt, NA_WIDTH), lambda b, t: (b, 0, 2)),
                  pl.BlockSpec(bias_tab.shape, lambda b, t: (0, 0, 0, 0))],
        out_specs=pl.BlockSpec((1, TILE, NA_WIDTH), lambda b, t: (b, t, 0)),
        out_shape=jax.ShapeDtypeStruct((nb, tt, NA_WIDTH), BF16),
        compiler_params=_params(2),
        name="attention",
    )(qkv, qkv, qkv, bias_tab)


def _rope_tables(seq):
    nf = DN_HEAD_DIM // 4
    inv = ROPE_THETA ** (-jnp.arange(nf, dtype=F32) / nf)
    t = jnp.arange(seq)
    lane = jnp.arange(DN_HEAD_DIM)
    pos = jnp.where(lane[None, :] < DN_HEAD_DIM // 2, (t // GRID_W)[:, None], (t % GRID_W)[:, None]).astype(F32)
    ang = pos * inv[lane % nf][None, :]
    first = (lane % (2 * nf)) < nf
    cos = jnp.cos(ang)
    sin = jnp.where(first[None, :], -jnp.sin(ang), jnp.sin(ang))
    cos = jnp.concatenate([jnp.ones((CTX_LEN, DN_HEAD_DIM), F32), cos], 0)
    sin = jnp.concatenate([jnp.zeros((CTX_LEN, DN_HEAD_DIM), F32), sin], 0)
    return cos, sin


def _dnprep_kernel(cur_ref, prev_ref, next_ref, cw_ref, cos_ref, sin_ref, q_ref, k_ref, v_ref, pad_ref):
    t = pl.program_id(1)
    nt = pl.num_programs(1)
    halo = prev_ref.shape[1]
    prev_ok = t >= 2
    next_ok = jnp.logical_and(t >= 1, t < nt - 1)
    pad_ref[0:halo] = jnp.where(prev_ok, prev_ref[0].astype(F32), 0.0)
    pad_ref[halo:halo + TILE] = cur_ref[0].astype(F32)
    pad_ref[halo + TILE:2 * halo + TILE] = jnp.where(next_ok, next_ref[0].astype(F32), 0.0)
    lane = lax.broadcasted_iota(jnp.int32, (1, LANES), 1)
    first = (lane % (DN_HEAD_DIM // 2)) < DN_HEAD_DIM // 4
    cos = cos_ref[...]
    sin = sin_ref[...]
    outs = (q_ref, k_ref, v_ref)
    for grp in range(3 * DN_HEADS):
        ls = slice(LANES * grp, LANES * (grp + 1))
        acc = None
        for j in range(DN_CONV):
            term = cw_ref[j:j + 1, ls] * pad_ref[pl.ds(halo - DN_CONV // 2 + j, TILE), ls]
            acc = term if acc is None else acc + term
        x = _silu(acc)
        which, head = divmod(grp, DN_HEADS)
        if which < 2:
            x = x * lax.rsqrt(jnp.sum(x * x, axis=-1, keepdims=True) + RMS_EPS)
            rot = jnp.where(first, pltpu.roll(x, LANES - DN_HEAD_DIM // 4, 1), pltpu.roll(x, DN_HEAD_DIM // 4, 1))
            x = x * cos + rot * sin
        if which == 0:
            x = x * DN_HEAD_DIM ** -0.5
        outs[which][0, :, LANES * head:LANES * (head + 1)] = x.astype(outs[which].dtype)


def _dn_prepare(qkv_dn, conv_w, cos, sin):
    nb, tt, w = qkv_dn.shape
    nt = tt // TILE
    halo = 16
    hpt = TILE // halo
    cw = jnp.zeros((8, w), F32).at[:DN_CONV].set(conv_w)
    out = pl.BlockSpec((1, TILE, DN_WIDTH), lambda b, t: (b, t, 0))
    return pl.pallas_call(
        _dnprep_kernel,
        grid=(nb, nt),
        in_specs=[pl.BlockSpec((1, TILE, w), lambda b, t: (b, t, 0)),
                  pl.BlockSpec((1, halo, w), lambda b, t: (b, jnp.maximum(t * hpt - 1, 0), 0)),
                  pl.BlockSpec((1, halo, w), lambda b, t: (b, jnp.minimum((t + 1) * hpt, nt * hpt - 1), 0)),
                  pl.BlockSpec((8, w), lambda b, t: (0, 0)),
                  pl.BlockSpec((TILE, DN_HEAD_DIM), lambda b, t: (t, 0)),
                  pl.BlockSpec((TILE, DN_HEAD_DIM), lambda b, t: (t, 0))],
        out_specs=[out, out, out],
        out_shape=[jax.ShapeDtypeStruct((nb, tt, DN_WIDTH), BF16)] * 3,
        scratch_shapes=[pltpu.VMEM((TILE + 2 * halo, w), F32)],
        compiler_params=_params(2),
        name="dn_prepare",
    )(qkv_dn, qkv_dn, qkv_dn, cw, cos, sin)


def _gate_kernel(a_ref, b_ref, alog_ref, dtb_ref, g_ref, beta_ref):
    g = -jnp.exp(alog_ref[...]) * _softplus(a_ref[0] + dtb_ref[...])
    beta_ref[0] = jax.nn.sigmoid(b_ref[0])
    ii = lax.broadcasted_iota(jnp.int32, (CHUNK, CHUNK), 0)
    jj = lax.broadcasted_iota(jnp.int32, (CHUNK, CHUNK), 1)
    g_ref[0, 0] = _dot(g[0], (ii <= jj).astype(F32), HIGHEST)
    g_ref[0, 1] = _dot(g[1], (ii >= jj).astype(F32), HIGHEST)


def _gates(ba, a_log, dt_bias):
    nb, tt, _ = ba.shape
    nc = tt // CHUNK
    nh = DN_HEADS
    rows = lambda x: x.transpose(0, 2, 1).reshape(nb, 2, nh * nc, CHUNK)
    spec = pl.BlockSpec((1, 2, nh * nc, CHUNK), lambda b: (b, 0, 0, 0))
    par = lambda p: jnp.broadcast_to(p.reshape(2, nh, 1, 1).astype(F32), (2, nh, nc, CHUNK)).reshape(2, nh * nc, CHUNK)
    pspec = pl.BlockSpec((2, nh * nc, CHUNK), lambda b: (0, 0, 0))
    gcum, beta = pl.pallas_call(
        _gate_kernel,
        grid=(nb,),
        in_specs=[spec, spec, pspec, pspec],
        out_specs=[spec, spec],
        out_shape=[jax.ShapeDtypeStruct((nb, 2, nh * nc, CHUNK), F32)] * 2,
        compiler_params=_params(1),
        name="dn_gates",
    )(rows(ba[..., 2 * nh:4 * nh]), rows(ba[..., 0:2 * nh]), par(a_log), par(dt_bias))
    return gcum.reshape(nb, 2 * nh, nc, CHUNK), beta.reshape(nb, 2 * nh, nc, CHUNK)


def _gdn_masks(d):
    width = DN_HEADS * CHUNK
    lane = lax.broadcasted_iota(jnp.int32, (CHUNK, width), 1)
    ri = lax.broadcasted_iota(jnp.int32, (CHUNK, width), 0)
    jj = lane % CHUNK
    lower = (ri >= jj) if d == 0 else (ri <= jj)
    strict = (ri > jj) if d == 0 else (ri < jj)
    return dict(hb=lane // CHUNK, lower=lower, strict=strict, eye=(ri == jj).astype(F32))


def _blockdiag(xp, hb):
    return jnp.concatenate([jnp.where(hb == h, xp, 0.0) for h in range(DN_HEADS)], axis=0).astype(BF16)


def _gdn_setup(q, k, v, col, row, d, mk):
    nh = DN_HEADS
    hb = mk["hb"]
    stack = lambda x: jnp.concatenate([x[:, LANES * h:LANES * (h + 1)] for h in range(nh)], axis=0)
    colb = lambda c, w: jnp.broadcast_to(col[:, c:c + 1], (CHUNK, w))
    ks, qs, vs = stack(k), stack(q), stack(v)
    gs = jnp.concatenate([colb(nh * d + h, LANES) for h in range(nh)], axis=0)
    bs = jnp.concatenate([colb(2 * nh + nh * d + h, LANES) for h in range(nh)], axis=0)
    eg = jnp.exp(gs)
    kbs = ks * bs
    rhs = jnp.concatenate([vs * bs, kbs * eg], axis=1).astype(BF16)
    full = _dot_nt(jnp.concatenate([kbs, qs], axis=0).astype(BF16), ks.astype(BF16))

    def pack(lo):
        acc = full[lo:lo + CHUNK]
        for h in range(1, nh):
            acc = jnp.where(hb == h, full[lo + CHUNK * h:lo + CHUNK * (h + 1)], acc)
        return acc

    cp = colb(nh * d, nh * CHUNK)
    for h in range(1, nh):
        cp = jnp.where(hb == h, colb(nh * d + h, nh * CHUNK), cp)
    decay = jnp.exp(jnp.where(mk["lower"], cp - row, NEG_BIG))
    m_p = -jnp.where(mk["strict"], pack(0) * decay, 0.0)
    last = CHUNK - 1 if d == 0 else 0
    g_last = [gs[CHUNK * h + last:CHUNK * h + last + 1] for h in range(nh)]
    return dict(m=m_p, p=mk["eye"] + m_p, rhs=rhs, qg=qs * eg, k=ks, g=gs, g_last=g_last,
                intra=_blockdiag(pack(nh * CHUNK) * decay, hb))


def _gdn_kernel(qf, kf, vf, colf, rowf, qb, kb, vb, colb, rowb, of_ref, ob_ref, s_ref):
    @pl.when(pl.program_id(1) == 0)
    def _reset():
        s_ref[...] = jnp.zeros_like(s_ref)

    nh = DN_HEADS
    srcs = ((qf, kf, vf, colf, rowf, of_ref), (qb, kb, vb, colb, rowb, ob_ref))
    masks = (_gdn_masks(0), _gdn_masks(1))
    jobs = [(d, step if d == 0 else CHUNKS_PER_TILE - 1 - step)
            for step in range(CHUNKS_PER_TILE) for d in range(2)]
    par = []
    for d, ch in jobs:
        q, k, v, col, row, _ = srcs[d]
        rows = slice(CHUNK * ch, CHUNK * (ch + 1))
        par.append(_gdn_setup(q[0, rows, :].astype(F32), k[0, rows, :].astype(F32), v[0, rows, :].astype(F32),
                              col[0, rows, :], row[0, ch, d:d + 1, :], d, masks[d]))

    span = 2
    while span < CHUNK:
        for (d, _), p in zip(jobs, par):
            p["m"] = _dot(p["m"].astype(BF16), _blockdiag(p["m"], masks[d]["hb"]))
        for (d, _), p in zip(jobs, par):
            p["p"] = p["p"] + _dot(p["p"].astype(BF16), _blockdiag(p["m"], masks[d]["hb"]))
        span *= 2
    for (d, _), p in zip(jobs, par):
        uw = _dot(_blockdiag(p["p"], masks[d]["hb"]), p["rhs"])
        p["u"], p["w"] = uw[:, 0:LANES], uw[:, LANES:2 * LANES]

    hs = lambda h: slice(CHUNK * h, CHUNK * (h + 1))
    for step in range(CHUNKS_PER_TILE):
        group = [(d, ch, p) for (d, ch), p in zip(jobs, par)][2 * step:2 * step + 2]
        chains = [(d, p, h) for d, _, p in group for h in range(nh)]
        states = [s_ref[nh * d + h] for d, _, h in chains]
        res = [_dot(jnp.concatenate([p["w"][hs(h)], p["qg"][hs(h)]], axis=0).astype(BF16), s.astype(BF16))
               for (_, p, h), s in zip(chains, states)]
        vnew = [p["u"][hs(h)] - r[0:CHUNK] for (_, p, h), r in zip(chains, res)]
        for (d, p, h), s, vn in zip(chains, states, vnew):
            g_last = p["g_last"][h]
            kg = (p["k"][hs(h)] * jnp.exp(g_last - p["g"][hs(h)])).astype(BF16)
            s_ref[nh * d + h] = s * jnp.exp(g_last) + _dot_tn(kg, vn.astype(BF16))
        for gi, (d, ch, p) in enumerate(group):
            o_intra = _dot(p["intra"], jnp.concatenate(vnew[nh * gi:nh * (gi + 1)], axis=0).astype(BF16))
            for h in range(nh):
                srcs[d][5][0, CHUNK * ch:CHUNK * (ch + 1), LANES * h:LANES * (h + 1)] = (
                    res[nh * gi + h][CHUNK:2 * CHUNK] + o_intra[hs(h)]).astype(srcs[d][5].dtype)


def _gdn(q, k, v, gcum, beta):
    nb, tt, _ = q.shape
    nt = tt // TILE
    nh = DN_HEADS
    col = jnp.concatenate([gcum, beta], axis=1).transpose(0, 2, 3, 1).reshape(nb, tt, 4 * nh)
    row = gcum.reshape(nb, 2, nh, tt // CHUNK, CHUNK).transpose(0, 3, 1, 2, 4).reshape(nb, tt // CHUNK, 2, nh * CHUNK)
    fwd = lambda b, t: (b, t, 0)
    bwd = lambda b, t: (b, jnp.where(t == 0, 0, nt - t), 0)
    tok = lambda w, im: pl.BlockSpec((1, TILE, w), im)
    rowspec = lambda im: pl.BlockSpec((1, CHUNKS_PER_TILE, 2, nh * CHUNK), lambda b, t: im(b, t) + (0,))
    side = lambda im: [tok(DN_WIDTH, im), tok(DN_WIDTH, im), tok(DN_WIDTH, im), tok(4 * nh, im), rowspec(im)]
    return pl.pallas_call(
        _gdn_kernel,
        grid=(nb, nt),
        in_specs=side(fwd) + side(bwd),
        out_specs=[tok(DN_WIDTH, fwd), tok(DN_WIDTH, bwd)],
        out_shape=[jax.ShapeDtypeStruct((nb, tt, DN_WIDTH), BF16)] * 2,
        scratch_shapes=[pltpu.VMEM((2 * nh, DN_HEAD_DIM, DN_HEAD_DIM), F32)],
        compiler_params=_params(2),
        name="gated_deltanet",
    )(q, k, v, col, row, q, k, v, col, row)


def _layernorm(x, g, b):
    mu = jnp.mean(x, axis=-1, keepdims=True)
    xc = x - mu
    var = jnp.mean(xc * xc, axis=-1, keepdims=True)
    return xc * lax.rsqrt(var + LN_EPS) * g + b


def _merge_kernel(alpha, with_router, ona_ref, of_ref, ob_ref, z_ref, x_ref, g1_ref, sh2_ref, sc2_ref,
                  nag_ref, dng_ref, wout_ref, lng_ref, lnb_ref, *rest):
    if with_router:
        router_ref, x1_ref, h2_ref, route_ref, count_ref, base_ref = rest
    else:
        x1_ref, h2_ref = rest
    ona = ona_ref[0].astype(F32)
    na = ona * lax.rsqrt(jnp.mean(ona * ona, axis=-1, keepdims=True) + RMS_EPS) * nag_ref[...]
    y = _dot(na.astype(BF16), wout_ref[0:NA_WIDTH, :])
    for h in range(DN_HEADS):
        ls = slice(LANES * h, LANES * (h + 1))
        od = of_ref[0, :, ls].astype(F32) + ob_ref[0, :, ls].astype(F32)
        dn = od * lax.rsqrt(jnp.mean(od * od, axis=-1, keepdims=True) + RMS_EPS) * dng_ref[...]
        dn = dn * _silu(z_ref[0, :, ls].astype(F32))
        y = y + _dot(dn.astype(BF16), wout_ref[NA_WIDTH + LANES * h:NA_WIDTH + LANES * (h + 1), :])
    x1 = _layernorm(alpha * x_ref[0] + g1_ref[0] * y, lng_ref[...], lnb_ref[...])
    x1_ref[0] = x1
    h2 = x1 * (1.0 + sc2_ref[0]) + sh2_ref[0]
    h2_ref[0] = h2.astype(h2_ref.dtype)
    if not with_router:
        return

    @pl.when(jnp.logical_and(pl.program_id(0) == 0, pl.program_id(1) == 0))
    def _reset():
        base_ref[...] = jnp.zeros_like(base_ref)

    h_hi = h2.astype(BF16)
    h_lo = (h2 - h_hi.astype(F32)).astype(BF16)
    logits = _dot(h_hi, router_ref[0]) + (_dot(h_lo, router_ref[0]) + _dot(h_hi, router_ref[1]))
    lane = lax.broadcasted_iota(jnp.int32, logits.shape, 1).astype(F32)
    lg = jnp.where(lane < N_EXPERTS, logits, NEG_BIG)
    m0 = jnp.max(lg, axis=-1, keepdims=True)
    e0 = jnp.min(jnp.where(lg == m0, lane, float(LANES)), axis=-1, keepdims=True)
    lg1 = jnp.where(lane == e0, NEG_BIG, lg)
    m1 = jnp.max(lg1, axis=-1, keepdims=True)
    e1 = jnp.min(jnp.where(lg1 == m1, lane, float(LANES)), axis=-1, keepdims=True)
    t1 = jnp.exp(m1 - m0)
    w0 = 1.0 / (1.0 + t1)
    w1 = t1 / (1.0 + t1)
    hot0 = lane == e0
    hot1 = lane == e1
    onehot = jnp.logical_or(hot0, hot1).astype(BF16)
    n = onehot.shape[0]
    earlier = (lax.broadcasted_iota(jnp.int32, (n, n), 0) > lax.broadcasted_iota(jnp.int32, (n, n), 1)).astype(BF16)
    before = base_ref[...] + _dot(earlier, onehot)
    rank0 = jnp.sum(jnp.where(hot0, before, 0.0), axis=-1, keepdims=True)
    rank1 = jnp.sum(jnp.where(hot1, before, 0.0), axis=-1, keepdims=True)
    total = base_ref[...] + jnp.sum(onehot.astype(F32), axis=0, keepdims=True)
    base_ref[...] = total
    count_ref[...] = jnp.broadcast_to(total, count_ref.shape)
    route = jnp.zeros_like(logits)
    for k, val in enumerate((e0, e1, w0, w1, rank0, rank1)):
        route = jnp.where(lane == k, val, route)
    route_ref[0] = route


def _merge(alpha, o_na, o_f, o_b, z, xs, mods, na_g, dn_g, w_out, ln_g, ln_b, router_p):
    nb, tt, _ = xs.shape
    nt = tt // TILE
    tok = lambda w: pl.BlockSpec((1, TILE, w), lambda b, t: (b, t, 0))
    vec = lambda w: pl.BlockSpec((1, w), lambda b, t: (0, 0))
    with_router = router_p is not None
    in_specs = [tok(NA_WIDTH), tok(DN_WIDTH), tok(DN_WIDTH), tok(DN_WIDTH), tok(D_MODEL),
                _mod_spec(2, nb), _mod_spec(3, nb), _mod_spec(4, nb), vec(NA_WIDTH), vec(DN_HEAD_DIM),
                pl.BlockSpec(w_out.shape, lambda b, t: (0, 0)), vec(D_MODEL), vec(D_MODEL)]
    args = [o_na, o_f, o_b, z, xs, mods, mods, mods, na_g.reshape(1, -1), dn_g.reshape(1, -1), w_out,
            ln_g.reshape(1, -1), ln_b.reshape(1, -1)]
    out_specs = [tok(D_MODEL), tok(D_MODEL)]
    out_shape = [jax.ShapeDtypeStruct((nb, tt, D_MODEL), F32),
                 jax.ShapeDtypeStruct((nb, tt, D_MODEL), F32 if with_router else BF16)]
    scratch = []
    if with_router:
        in_specs.append(pl.BlockSpec(router_p.shape, lambda b, t: (0, 0, 0)))
        args.append(router_p)
        out_specs += [tok(LANES), pl.BlockSpec((8, LANES), lambda b, t: (0, 0))]
        out_shape += [jax.ShapeDtypeStruct((nb, tt, LANES), F32), jax.ShapeDtypeStruct((8, LANES), F32)]
        scratch = [pltpu.VMEM((1, LANES), F32)]
    return pl.pallas_call(
        functools.partial(_merge_kernel, alpha, with_router),
        grid=(nb, nt),
        in_specs=in_specs,
        out_specs=out_specs,
        out_shape=out_shape,
        scratch_shapes=scratch,
        compiler_params=_params(2),
        name="merge_router" if with_router else "merge",
    )(*args)


def _resid_ln_kernel(alpha, x_ref, f_ref, g2_ref, lng_ref, lnb_ref, o_ref):
    o_ref[0] = _layernorm(alpha * x_ref[0] + g2_ref[0] * f_ref[0], lng_ref[...], lnb_ref[...])


def _resid_ln(alpha, xs, f, mods, ln_g, ln_b):
    nb, tt, _ = xs.shape
    tok = pl.BlockSpec((1, TILE, D_MODEL), lambda b, t: (b, t, 0))
    vec = pl.BlockSpec((1, D_MODEL), lambda b, t: (0, 0))
    return pl.pallas_call(
        functools.partial(_resid_ln_kernel, alpha),
        grid=(nb, tt // TILE),
        in_specs=[tok, tok, _mod_spec(5, nb), vec, vec],
        out_specs=tok,
        out_shape=jax.ShapeDtypeStruct(xs.shape, F32),
        compiler_params=_params(2),
        name="residual_layernorm",
    )(xs, f, mods, ln_g.reshape(1, -1), ln_b.reshape(1, -1))


def _ffn_kernel(te_ref, nact_ref, x_ref, wg_ref, wu_ref, wd_ref, o_ref, act_ref):
    del te_ref
    f = pl.program_id(1)
    active = pl.program_id(0) < nact_ref[0]

    @pl.when(jnp.logical_not(active))
    def _unused_tile():
        o_ref[...] = jnp.zeros_like(o_ref)

    @pl.when(active)
    def _active():
        x = x_ref[...].astype(BF16)
        tf = act_ref.shape[1]
        off = 0
        while off < tf:
            n = min(FFN_SUB, tf - off)
            gt = _dot(x, wg_ref[0, :, off:off + n])
            up = _dot(x, wu_ref[0, :, off:off + n])
            act_ref[:, off:off + n] = (_silu(gt) * up).astype(BF16)
            off += n
        part = _dot(act_ref[...], wd_ref[0])

        @pl.when(f == 0)
        def _first():
            o_ref[...] = part

        @pl.when(f > 0)
        def _rest():
            o_ref[...] += part


def _ffn(x, tile_expert, n_active, wg, wu, wd, tm, tf):
    s, dm = x.shape
    ff = wg.shape[2]
    nf = ff // tf
    fblk = lambda i, f, na: jnp.where(i < na[0], f, nf - 1)
    grid_spec = pltpu.PrefetchScalarGridSpec(
        num_scalar_prefetch=2,
        grid=(s // tm, nf),
        in_specs=[pl.BlockSpec((tm, dm), lambda i, f, te, na: (i, 0)),
                  pl.BlockSpec((1, dm, tf), lambda i, f, te, na: (te[i], 0, fblk(i, f, na))),
                  pl.BlockSpec((1, dm, tf), lambda i, f, te, na: (te[i], 0, fblk(i, f, na))),
                  pl.BlockSpec((1, tf, dm), lambda i, f, te, na: (te[i], fblk(i, f, na), 0))],
        out_specs=pl.BlockSpec((tm, dm), lambda i, f, te, na: (i, 0)),
        scratch_shapes=[pltpu.VMEM((tm, tf), BF16)],
    )
    return pl.pallas_call(
        _ffn_kernel,
        grid_spec=grid_spec,
        out_shape=jax.ShapeDtypeStruct((s, dm), F32),
        compiler_params=_params(2),
        name="swiglu",
    )(tile_expert, n_active, x, wg, wu, wd)


def _dense_ffn(h2, wg, wu, wd):
    nb, tt, dm = h2.shape
    s = nb * tt
    tm = FFN_TM if s % FFN_TM == 0 else TILE
    ff = wg.shape[1]
    tf = ff // 2 if (ff // 2) % LANES == 0 else ff
    n_tiles = s // tm
    out = _ffn(h2.reshape(s, dm), jnp.zeros((n_tiles,), jnp.int32), jnp.full((1,), n_tiles, jnp.int32),
               wg[None], wu[None], wd[None], tm, tf)
    return out.reshape(nb, tt, dm)


def _row_copy(src_ref, src_row, dst_ref, dst_row, sem):
    return pltpu.make_async_copy(src_ref.at[pl.ds(src_row, 1)], dst_ref.at[pl.ds(dst_row, 1)], sem)


DMA_UNROLL = 8


def _dispatch_kernel(tm, meta_ref, slot_ref, prev_slot_ref, h_ref, xs_ref, zero_ref, sem):
    step = pl.program_id(0)
    n_tiles = xs_ref.shape[0] // tm

    @pl.when(step == 0)
    def _zero_unfilled():
        zero_ref[...] = jnp.zeros_like(zero_ref)

        def zero_tile(tile):
            cp = pltpu.make_async_copy(zero_ref, xs_ref.at[pl.ds(pl.multiple_of(tile * tm, tm), tm)], sem)
            cp.start()
            cp.wait()

        for e in range(N_EXPERTS):
            cnt = meta_ref[e]

            @pl.when(cnt % tm != 0)
            def _():
                zero_tile(meta_ref[N_EXPERTS + e] + cnt // tm)

        def tail(tile, carry):
            zero_tile(tile)
            return carry
        lax.fori_loop(meta_ref[2 * N_EXPERTS], n_tiles, tail, 0)

    def copies(slots, tile, i):
        return [_row_copy(h_ref, tile * TILE + i, xs_ref, slots[TOP_K * i + k], sem) for k in range(TOP_K)]

    def issue(i, carry):
        for cp in copies(slot_ref, step, i):
            cp.start()
        return carry

    def drain(slots, tile):
        def body(i, carry):
            for cp in copies(slots, tile, i):
                cp.wait()
            return carry
        lax.fori_loop(0, TILE, body, 0, unroll=DMA_UNROLL)

    lax.fori_loop(0, TILE, issue, 0, unroll=DMA_UNROLL)

    @pl.when(step > 0)
    def _previous():
        drain(prev_slot_ref, step - 1)

    @pl.when(step == pl.num_programs(0) - 1)
    def _last():
        drain(slot_ref, step)


def _dispatch(h2, slots, meta, tm, n_tiles):
    s, dm = h2.shape
    slot_block = lambda im: pl.BlockSpec((TOP_K * TILE,), im, memory_space=pltpu.SMEM)
    grid_spec = pltpu.PrefetchScalarGridSpec(
        num_scalar_prefetch=1,
        grid=(s // TILE,),
        in_specs=[slot_block(lambda i, meta: (i,)), slot_block(lambda i, meta: (jnp.maximum(i - 1, 0),)),
                  pl.BlockSpec(memory_space=pl.ANY)],
        out_specs=pl.BlockSpec(memory_space=pl.ANY),
        scratch_shapes=[pltpu.VMEM((tm, dm), F32), pltpu.SemaphoreType.DMA(())],
    )
    return pl.pallas_call(
        functools.partial(_dispatch_kernel, tm),
        grid_spec=grid_spec,
        out_shape=jax.ShapeDtypeStruct((n_tiles * tm, dm), F32),
        compiler_params=_params(1),
        name="moe_dispatch",
    )(meta, slots, slots, h2)


def _combine_kernel(alpha, slot_ref, next_slot_ref, x_ref, route_ref, g2_ref, lng_ref, lnb_ref, y_ref, o_ref,
                    buf_ref, sems):
    step = pl.program_id(0) * pl.num_programs(1) + pl.program_id(1)
    n_steps = pl.num_programs(0) * pl.num_programs(1)
    cur = step % 2

    def copies(slots, half, i):
        return [_row_copy(y_ref, slots[TOP_K * i + k], buf_ref.at[half, k], i, sems.at[half]) for k in range(TOP_K)]

    def issue(slots, half):
        def body(i, carry):
            for cp in copies(slots, half, i):
                cp.start()
            return carry
        lax.fori_loop(0, TILE, body, 0, unroll=DMA_UNROLL)

    @pl.when(step == 0)
    def _first():
        issue(slot_ref, cur)

    @pl.when(step + 1 < n_steps)
    def _prefetch():
        issue(next_slot_ref, 1 - cur)

    def drain(i, carry):
        for cp in copies(slot_ref, cur, i):
            cp.wait()
        return carry
    lax.fori_loop(0, TILE, drain, 0, unroll=DMA_UNROLL)

    route = route_ref[0]
    f = route[:, 2:3] * buf_ref[cur, 0] + route[:, 3:4] * buf_ref[cur, 1]
    o_ref[0] = _layernorm(alpha * x_ref[0] + g2_ref[0] * f, lng_ref[...], lnb_ref[...])


def _combine_ln(alpha, x1, route, slots, y, mods, ln_g, ln_b):
    nb, tt, dm = x1.shape
    nt = tt // TILE
    tok = lambda w: pl.BlockSpec((1, TILE, w), lambda b, t: (b, t, 0))
    vec = pl.BlockSpec((1, dm), lambda b, t: (0, 0))
    slot_block = lambda im: pl.BlockSpec((TOP_K * TILE,), im, memory_space=pltpu.SMEM)
    return pl.pallas_call(
        functools.partial(_combine_kernel, alpha),
        grid=(nb, nt),
        in_specs=[slot_block(lambda b, t: (b * nt + t,)),
                  slot_block(lambda b, t: (jnp.minimum(b * nt + t + 1, nb * nt - 1),)),
                  tok(dm), tok(LANES), _mod_spec(5, nb), vec, vec,
                  pl.BlockSpec(memory_space=pl.ANY)],
        out_specs=tok(dm),
        out_shape=jax.ShapeDtypeStruct(x1.shape, F32),
        scratch_shapes=[pltpu.VMEM((2, TOP_K, TILE, dm), F32), pltpu.SemaphoreType.DMA((2,))],
        compiler_params=_params(2),
        name="moe_combine_layernorm",
    )(slots, slots, x1, route, mods, ln_g.reshape(1, -1), ln_b.reshape(1, -1), y)


def _moe_layer(alpha, x1, h2, route, counts, mods, wg, wu, wd, ln_g, ln_b):
    nb, tt, dm = x1.shape
    s = nb * tt
    tm = FFN_TM
    ne = N_EXPERTS
    cnt = counts[0, :ne].astype(jnp.int32)
    tiles = (cnt + tm - 1) // tm
    ends = jnp.cumsum(tiles)
    first = ends - tiles
    n_active = ends[ne - 1:ne]
    n_tiles = (TOP_K * s + ne * (tm - 1)) // tm
    r = route.reshape(s, LANES)
    chosen = r[:, 0:TOP_K].astype(jnp.int32)[..., None] == jnp.arange(ne)
    slots = (jnp.sum(jnp.where(chosen, first * tm, 0), axis=-1) + r[:, 4:4 + TOP_K].astype(jnp.int32)).reshape(-1)
    i = jnp.minimum(jnp.arange(n_tiles), n_active - 1)
    te = jnp.sum(i[:, None] >= ends[None, :], axis=1).astype(jnp.int32)
    xs = _dispatch(h2.reshape(s, dm), slots, jnp.concatenate([cnt, first, n_active]), tm, n_tiles)
    y = _ffn(xs, te, n_active, wg, wu, wd, tm, wg.shape[2] // 2)
    return _combine_ln(alpha, x1, route, slots, y, mods, ln_g, ln_b)


def kernel(x, c, ctx, c_ctx, w_mod, b_mod, w_in, dn_conv_w, dn_a_log, dn_dt_bias, dn_norm_g, na_rpb, na_out_g, w_out, ln1_g, ln1_b, ln2_g, ln2_b, ffn_w_gate, ffn_w_up, ffn_w_down, moe_router, moe_w_gate, moe_w_up, moe_w_down):
    nb, seq, dm = x.shape
    depth = w_mod.shape[0]
    alpha = (2.0 * depth) ** 0.25
    n_ctx = ctx.shape[1]
    assert n_ctx == CTX_LEN == TILE and seq % TILE == 0 and dm == D_MODEL

    xs = jnp.concatenate([ctx, x], axis=1)
    rows = -(-(nb + 1) // 8) * 8
    c_all = jnp.zeros((rows, dm), F32).at[:nb].set(c).at[nb].set(c_ctx)
    mods_all = _modulation(c_all, w_mod, b_mod)
    cos, sin = _rope_tables(seq)
    p_in = w_in.shape[2]
    p_pad = 3 * NA_WIDTH + 4 * DN_WIDTH + LANES

    for l in range(depth):
        mods = mods_all[l].reshape(rows, 1, 6 * dm)
        w_in_p = jnp.zeros((dm, p_pad), BF16).at[:, :p_in].set(w_in[l].astype(BF16))
        qkv_na, qkv_dn, z, ba = _in_projection(xs, mods, w_in_p)
        o_na = _attention(qkv_na, _na_bias_table(na_rpb[l]))
        q_dn, k_dn, v_dn = _dn_prepare(qkv_dn, dn_conv_w[l], cos, sin)
        gcum, beta = _gates(ba, dn_a_log[l], dn_dt_bias[l])
        o_f, o_b = _gdn(q_dn, k_dn, v_dn, gcum, beta)
        moe = l % 2 == 1
        i = l // 2
        router_p = None
        if moe:
            r_full = jnp.zeros((dm, LANES), F32).at[:, :N_EXPERTS].set(moe_router[i])
            r_hi = r_full.astype(BF16)
            router_p = jnp.stack([r_hi, (r_full - r_hi.astype(F32)).astype(BF16)])
        merged = _merge(alpha, o_na, o_f, o_b, z, xs, mods, na_out_g[l], dn_norm_g[l], w_out[l].astype(BF16),
                        ln1_g[l], ln1_b[l], router_p)
        if moe:
            x1, h2, route, counts = merged
            xs = _moe_layer(alpha, x1, h2, route, counts, mods, moe_w_gate[i].astype(BF16),
                            moe_w_up[i].astype(BF16), moe_w_down[i].astype(BF16), ln2_g[l], ln2_b[l])
        else:
            x1, h2 = merged
            f = _dense_ffn(h2, ffn_w_gate[i].astype(BF16), ffn_w_up[i].astype(BF16), ffn_w_down[i].astype(BF16))
            xs = _resid_ln(alpha, x1, f, mods, ln2_g[l], ln2_b[l])
    return xs[:, n_ctx:]
```

```python
import functools

import jax
import jax.numpy as jnp
from jax import lax
from jax.experimental import pallas as pl
from jax.experimental.pallas import tpu as pltpu

F32 = jnp.float32
BF16 = jnp.bfloat16
HIGHEST = lax.Precision.HIGHEST

D_MODEL = 1024
GRID_W = 64
CTX_LEN = 256
NA_HEADS = 8
NA_HEAD_DIM = 64
NA_WIDTH = NA_HEADS * NA_HEAD_DIM
NA_KR = 8
NA_KC = 16
DN_HEAD_DIM = 128
DN_HEADS = 4
DN_WIDTH = DN_HEADS * DN_HEAD_DIM
DN_CONV = 5
CHUNK = 64
ROPE_THETA = 10000.0
N_EXPERTS = 8
TOP_K = 2
LN_EPS = 1e-5
RMS_EPS = 1e-6
NEG_BIG = -1e30

TILE = 256
CHUNKS_PER_TILE = TILE // CHUNK
ROWS_PER_TILE = TILE // GRID_W
LANES = 128
VMEM_LIMIT = 56 * 1024 * 1024
FFN_TM = 1024
FFN_SUB = 256


def _params(n_axes, vmem=VMEM_LIMIT):
    return pltpu.CompilerParams(dimension_semantics=("arbitrary",) * n_axes, vmem_limit_bytes=vmem)


def _dot(a, b, precision=None):
    return jnp.dot(a, b, preferred_element_type=F32, precision=precision)


def _dot_nt(a, b):
    return lax.dot_general(a, b, (((1,), (1,)), ((), ())), preferred_element_type=F32)


def _dot_tn(a, b):
    return lax.dot_general(a, b, (((0,), (0,)), ((), ())), preferred_element_type=F32)


def _silu(x):
    return x * jax.nn.sigmoid(x)


def _softplus(x):
    return jnp.maximum(x, 0.0) + jnp.log1p(jnp.exp(-jnp.abs(x)))


def _mod_spec(k, nb):
    return pl.BlockSpec((1, 1, D_MODEL), lambda b, t: (jnp.where(t == 0, nb, b), 0, k))


def _mod_kernel(c_ref, w_ref, b_ref, o_ref):
    o_ref[0] = _dot(_silu(c_ref[...]), w_ref[0], HIGHEST) + b_ref[0]


def _modulation(c_all, w_mod, b_mod):
    depth, _, n = w_mod.shape
    rows = c_all.shape[0]
    tn = 1536
    return pl.pallas_call(
        _mod_kernel,
        grid=(depth, n // tn),
        in_specs=[pl.BlockSpec((rows, D_MODEL), lambda l, j: (0, 0)),
                  pl.BlockSpec((1, D_MODEL, tn), lambda l, j: (l, 0, j)),
                  pl.BlockSpec((1, 1, tn), lambda l, j: (l, 0, j))],
        out_specs=pl.BlockSpec((1, rows, tn), lambda l, j: (l, 0, j)),
        out_shape=jax.ShapeDtypeStruct((depth, rows, n), F32),
        compiler_params=_params(2),
        name="modulation",
    )(c_all, w_mod, b_mod.reshape(depth, 1, n))


def _inproj_kernel(x_ref, sh_ref, sc_ref, w_ref, na_ref, dn_ref, z_ref, ba_ref):
    h = (x_ref[0] * (1.0 + sc_ref[0]) + sh_ref[0]).astype(BF16)
    na = _dot(h, w_ref[:, 0:3 * NA_WIDTH])
    col = lax.broadcasted_iota(jnp.int32, (1, 3 * NA_WIDTH), 1)
    na_ref[0] = (na * jnp.where(col < NA_WIDTH, NA_HEAD_DIM ** -0.5, 1.0)).astype(BF16)
    o = 3 * NA_WIDTH
    dn_ref[0] = _dot(h, w_ref[:, o:o + 3 * DN_WIDTH]).astype(dn_ref.dtype)
    o += 3 * DN_WIDTH
    z_ref[0] = _dot(h, w_ref[:, o:o + DN_WIDTH]).astype(z_ref.dtype)
    o += DN_WIDTH
    ba_ref[0] = _dot(h, w_ref[:, o:o + LANES])


def _in_projection(xs, mods, w_in_p):
    nb, tt, _ = xs.shape
    nt = tt // TILE
    tok = lambda w: pl.BlockSpec((1, TILE, w), lambda b, t: (b, t, 0))
    return pl.pallas_call(
        _inproj_kernel,
        grid=(nb, nt),
        in_specs=[tok(D_MODEL), _mod_spec(0, nb), _mod_spec(1, nb),
                  pl.BlockSpec(w_in_p.shape, lambda b, t: (0, 0))],
        out_specs=[tok(3 * NA_WIDTH), tok(3 * DN_WIDTH), tok(DN_WIDTH), tok(LANES)],
        out_shape=[jax.ShapeDtypeStruct((nb, tt, 3 * NA_WIDTH), BF16),
                   jax.ShapeDtypeStruct((nb, tt, 3 * DN_WIDTH), BF16),
                   jax.ShapeDtypeStruct((nb, tt, DN_WIDTH), BF16),
                   jax.ShapeDtypeStruct((nb, tt, LANES), F32)],
        compiler_params=_params(2),
        name="in_projection",
    )(xs, mods, mods, w_in_p)


def _na_bias_table(rpb):
    qc = jnp.arange(GRID_W)[:, None]
    kc = jnp.arange(GRID_W)[None, :]
    cstart = jnp.clip(qc - NA_KC // 2, 0, GRID_W - NA_KC)
    ok = (kc >= cstart) & (kc < cstart + NA_KC)
    left = GRID_W - NA_KC
    rp = jnp.pad(rpb, ((0, 0), (0, 0), (left, left)))
    toe = jnp.stack([rp[:, :, GRID_W - 1 - q:2 * GRID_W - 1 - q] for q in range(GRID_W)], axis=2)
    toe = jnp.where(ok, toe, NEG_BIG)
    tabs = [toe[:, NA_KR - 1 - v:2 * NA_KR - 1 - v].transpose(0, 2, 1, 3).reshape(NA_HEADS, GRID_W, NA_KR * GRID_W)
            for v in range(NA_KR)]
    return jnp.stack(tabs).reshape(NA_KR, NA_HEADS // 2, 2 * GRID_W, NA_KR * GRID_W)


def _attn_kernel(q_ref, k_ref, v_ref, bias_ref, o_ref):
    t = pl.program_id(1)
    lo = lax.broadcasted_iota(jnp.int32, (1, LANES), 1) < NA_HEAD_DIM
    hi = jnp.logical_not(lo)
    n_rows = (k_ref.shape[1] - CTX_LEN) // GRID_W
    n_keys = NA_KR * GRID_W

    def split_heads(qp):
        zero = jnp.zeros_like(qp)
        return jnp.concatenate([jnp.where(lo, qp, zero), jnp.where(hi, qp, zero)], axis=0)

    def join_heads(o, m):
        return jnp.where(lo, o[0:m], o[m:2 * m])

    @pl.when(t == 0)
    def _context():
        for pair in range(NA_HEADS // 2):
            ls = slice(LANES * pair, LANES * (pair + 1))
            s = _dot_nt(split_heads(q_ref[0, :, ls]), k_ref[0, 0:CTX_LEN, ls])
            p = jnp.exp(s - jnp.max(s, axis=-1, keepdims=True))
            o = _dot(p.astype(BF16), v_ref[0, 0:CTX_LEN, ls]) / jnp.sum(p, axis=-1, keepdims=True)
            o_ref[0, :, ls] = join_heads(o, TILE).astype(o_ref.dtype)

    @pl.when(t > 0)
    def _latent():
        for pair in range(NA_HEADS // 2):
            ls = slice(LANES * pair, LANES * (pair + 1))
            q2 = split_heads(q_ref[0, :, ls])
            s_ctx = _dot_nt(q2, k_ref[0, 0:CTX_LEN, ls])
            p_ctx, dens, o_loc = [], [], []
            for rr in range(ROWS_PER_TILE):
                r = (t - 1) * ROWS_PER_TILE + rr
                rs = jnp.clip(r - NA_KR // 2, 0, n_rows - NA_KR)
                keys = pl.ds(pl.multiple_of(CTX_LEN + rs * GRID_W, GRID_W), n_keys)
                row = lambda x: jnp.concatenate([x[GRID_W * rr:GRID_W * (rr + 1)],
                                                 x[TILE + GRID_W * rr:TILE + GRID_W * (rr + 1)]], axis=0)
                s_l = _dot_nt(row(q2), k_ref[0, keys, ls]) + bias_ref[r - rs, pair]
                s_c = row(s_ctx)
                mx = jnp.maximum(jnp.max(s_l, axis=-1, keepdims=True), jnp.max(s_c, axis=-1, keepdims=True))
                p_l = jnp.exp(s_l - mx)
                p_c = jnp.exp(s_c - mx)
                dens.append(jnp.sum(p_l, axis=-1, keepdims=True) + jnp.sum(p_c, axis=-1, keepdims=True))
                o_loc.append(_dot(p_l.astype(BF16), v_ref[0, keys, ls]))
                p_ctx.append(p_c.astype(BF16))
            o_ctx = _dot(jnp.concatenate(p_ctx, axis=0), v_ref[0, 0:CTX_LEN, ls])
            for rr in range(ROWS_PER_TILE):
                o = (o_loc[rr] + o_ctx[2 * GRID_W * rr:2 * GRID_W * (rr + 1)]) / dens[rr]
                o_ref[0, GRID_W * rr:GRID_W * (rr + 1), ls] = join_heads(o, GRID_W).astype(o_ref.dtype)


def _attention(qkv, bias_tab):
    nb, tt, _ = qkv.shape
    nt = tt // TILE
    return pl.pallas_call(
        _attn_kernel,
        grid=(nb, nt),
        in_specs=[pl.BlockSpec((1, TILE, NA_WIDTH), lambda b, t: (b, t, 0)),
                  pl.BlockSpec((1, tt, NA_WIDTH), lambda b, t: (b, 0, 1)),
                  pl.BlockSpec((1, tt, NA_WIDTH), lambda b, t: (b, 0, 2)),
                  pl.BlockSpec(bias_tab.shape, lambda b, t: (0, 0, 0, 0))],
        out_specs=pl.BlockSpec((1, TILE, NA_WIDTH), lambda b, t: (b, t, 0)),
        out_shape=jax.ShapeDtypeStruct((nb, tt, NA_WIDTH), BF16),
        compiler_params=_params(2),
        name="attention",
    )(qkv, qkv, qkv, bias_tab)


def _rope_tables(seq):
    nf = DN_HEAD_DIM // 4
    inv = ROPE_THETA ** (-jnp.arange(nf, dtype=F32) / nf)
    t = jnp.arange(seq)
    lane = jnp.arange(DN_HEAD_DIM)
    pos = jnp.where(lane[None, :] < DN_HEAD_DIM // 2, (t // GRID_W)[:, None], (t % GRID_W)[:, None]).astype(F32)
    ang = pos * inv[lane % nf][None, :]
    first = (lane % (2 * nf)) < nf
    cos = jnp.cos(ang)
    sin = jnp.where(first[None, :], -jnp.sin(ang), jnp.sin(ang))
    cos = jnp.concatenate([jnp.ones((CTX_LEN, DN_HEAD_DIM), F32), cos], 0)
    sin = jnp.concatenate([jnp.zeros((CTX_LEN, DN_HEAD_DIM), F32), sin], 0)
    return cos, sin


def _dnprep_kernel(cur_ref, prev_ref, next_ref, cw_ref, cos_ref, sin_ref, q_ref, k_ref, v_ref, pad_ref):
    t = pl.program_id(1)
    nt = pl.num_programs(1)
    halo = prev_ref.shape[1]
    prev_ok = t >= 2
    next_ok = jnp.logical_and(t >= 1, t < nt - 1)
    pad_ref[0:halo] = jnp.where(prev_ok, prev_ref[0].astype(F32), 0.0)
    pad_ref[halo:halo + TILE] = cur_ref[0].astype(F32)
    pad_ref[halo + TILE:2 * halo + TILE] = jnp.where(next_ok, next_ref[0].astype(F32), 0.0)
    lane = lax.broadcasted_iota(jnp.int32, (1, LANES), 1)
    first = (lane % (DN_HEAD_DIM // 2)) < DN_HEAD_DIM // 4
    cos = cos_ref[...]
    sin = sin_ref[...]
    outs = (q_ref, k_ref, v_ref)
    for grp in range(3 * DN_HEADS):
        ls = slice(LANES * grp, LANES * (grp + 1))
        acc = None
        for j in range(DN_CONV):
            term = cw_ref[j:j + 1, ls] * pad_ref[pl.ds(halo - DN_CONV // 2 + j, TILE), ls]
            acc = term if acc is None else acc + term
        x = _silu(acc)
        which, head = divmod(grp, DN_HEADS)
        if which < 2:
            x = x * lax.rsqrt(jnp.sum(x * x, axis=-1, keepdims=True) + RMS_EPS)
            rot = jnp.where(first, pltpu.roll(x, LANES - DN_HEAD_DIM // 4, 1), pltpu.roll(x, DN_HEAD_DIM // 4, 1))
            x = x * cos + rot * sin
        if which == 0:
            x = x * DN_HEAD_DIM ** -0.5
        outs[which][0, :, LANES * head:LANES * (head + 1)] = x.astype(outs[which].dtype)


def _dn_prepare(qkv_dn, conv_w, cos, sin):
    nb, tt, w = qkv_dn.shape
    nt = tt // TILE
    halo = 16
    hpt = TILE // halo
    cw = jnp.zeros((8, w), F32).at[:DN_CONV].set(conv_w)
    out = pl.BlockSpec((1, TILE, DN_WIDTH), lambda b, t: (b, t, 0))
    return pl.pallas_call(
        _dnprep_kernel,
        grid=(nb, nt),
        in_specs=[pl.BlockSpec((1, TILE, w), lambda b, t: (b, t, 0)),
                  pl.BlockSpec((1, halo, w), lambda b, t: (b, jnp.maximum(t * hpt - 1, 0), 0)),
                  pl.BlockSpec((1, halo, w), lambda b, t: (b, jnp.minimum((t + 1) * hpt, nt * hpt - 1), 0)),
                  pl.BlockSpec((8, w), lambda b, t: (0, 0)),
                  pl.BlockSpec((TILE, DN_HEAD_DIM), lambda b, t: (t, 0)),
                  pl.BlockSpec((TILE, DN_HEAD_DIM), lambda b, t: (t, 0))],
        out_specs=[out, out, out],
        out_shape=[jax.ShapeDtypeStruct((nb, tt, DN_WIDTH), BF16)] * 3,
        scratch_shapes=[pltpu.VMEM((TILE + 2 * halo, w), F32)],
        compiler_params=_params(2),
        name="dn_prepare",
    )(qkv_dn, qkv_dn, qkv_dn, cw, cos, sin)


def _gate_kernel(a_ref, b_ref, alog_ref, dtb_ref, g_ref, beta_ref):
    g = -jnp.exp(alog_ref[...]) * _softplus(a_ref[0] + dtb_ref[...])
    beta_ref[0] = jax.nn.sigmoid(b_ref[0])
    ii = lax.broadcasted_iota(jnp.int32, (CHUNK, CHUNK), 0)
    jj = lax.broadcasted_iota(jnp.int32, (CHUNK, CHUNK), 1)
    g_ref[0, 0] = _dot(g[0], (ii <= jj).astype(F32), HIGHEST)
    g_ref[0, 1] = _dot(g[1], (ii >= jj).astype(F32), HIGHEST)


def _gates(ba, a_log, dt_bias):
    nb, tt, _ = ba.shape
    nc = tt // CHUNK
    nh = DN_HEADS
    rows = lambda x: x.transpose(0, 2, 1).reshape(nb, 2, nh * nc, CHUNK)
    spec = pl.BlockSpec((1, 2, nh * nc, CHUNK), lambda b: (b, 0, 0, 0))
    par = lambda p: jnp.broadcast_to(p.reshape(2, nh, 1, 1).astype(F32), (2, nh, nc, CHUNK)).reshape(2, nh * nc, CHUNK)
    pspec = pl.BlockSpec((2, nh * nc, CHUNK), lambda b: (0, 0, 0))
    gcum, beta = pl.pallas_call(
        _gate_kernel,
        grid=(nb,),
        in_specs=[spec, spec, pspec, pspec],
        out_specs=[spec, spec],
        out_shape=[jax.ShapeDtypeStruct((nb, 2, nh * nc, CHUNK), F32)] * 2,
        compiler_params=_params(1),
        name="dn_gates",
    )(rows(ba[..., 2 * nh:4 * nh]), rows(ba[..., 0:2 * nh]), par(a_log), par(dt_bias))
    return gcum.reshape(nb, 2 * nh, nc, CHUNK), beta.reshape(nb, 2 * nh, nc, CHUNK)


def _gdn_masks(d):
    width = DN_HEADS * CHUNK
    lane = lax.broadcasted_iota(jnp.int32, (CHUNK, width), 1)
    ri = lax.broadcasted_iota(jnp.int32, (CHUNK, width), 0)
    jj = lane % CHUNK
    lower = (ri >= jj) if d == 0 else (ri <= jj)
    strict = (ri > jj) if d == 0 else (ri < jj)
    return dict(hb=lane // CHUNK, lower=lower, strict=strict, eye=(ri == jj).astype(F32))


def _blockdiag(xp, hb):
    return jnp.concatenate([jnp.where(hb == h, xp, 0.0) for h in range(DN_HEADS)], axis=0).astype(BF16)


def _gdn_setup(q, k, v, col, row, d, mk):
    nh = DN_HEADS
    hb = mk["hb"]
    stack = lambda x: jnp.concatenate([x[:, LANES * h:LANES * (h + 1)] for h in range(nh)], axis=0)
    colb = lambda c, w: jnp.broadcast_to(col[:, c:c + 1], (CHUNK, w))
    ks, qs, vs = stack(k), stack(q), stack(v)
    gs = jnp.concatenate([colb(nh * d + h, LANES) for h in range(nh)], axis=0)
    bs = jnp.concatenate([colb(2 * nh + nh * d + h, LANES) for h in range(nh)], axis=0)
    eg = jnp.exp(gs)
    kbs = ks * bs
    rhs = jnp.concatenate([vs * bs, kbs * eg], axis=1).astype(BF16)
    full = _dot_nt(jnp.concatenate([kbs, qs], axis=0).astype(BF16), ks.astype(BF16))

    def pack(lo):
        acc = full[lo:lo + CHUNK]
        for h in range(1, nh):
            acc = jnp.where(hb == h, full[lo + CHUNK * h:lo + CHUNK * (h + 1)], acc)
        return acc

    cp = colb(nh * d, nh * CHUNK)
    for h in range(1, nh):
        cp = jnp.where(hb == h, colb(nh * d + h, nh * CHUNK), cp)
    decay = jnp.exp(jnp.where(mk["lower"], cp - row, NEG_BIG))
    m_p = -jnp.where(mk["strict"], pack(0) * decay, 0.0)
    last = CHUNK - 1 if d == 0 else 0
    g_last = [gs[CHUNK * h + last:CHUNK * h + last + 1] for h in range(nh)]
    return dict(m=m_p, p=mk["eye"] + m_p, rhs=rhs, qg=qs * eg, k=ks, g=gs, g_last=g_last,
                intra=_blockdiag(pack(nh * CHUNK) * decay, hb))


def _gdn_kernel(qf, kf, vf, colf, rowf, qb, kb, vb, colb, rowb, of_ref, ob_ref, s_ref):
    @pl.when(pl.program_id(1) == 0)
    def _reset():
        s_ref[...] = jnp.zeros_like(s_ref)

    nh = DN_HEADS
    srcs = ((qf, kf, vf, colf, rowf, of_ref), (qb, kb, vb, colb, rowb, ob_ref))
    masks = (_gdn_masks(0), _gdn_masks(1))
    jobs = [(d, step if d == 0 else CHUNKS_PER_TILE - 1 - step)
            for step in range(CHUNKS_PER_TILE) for d in range(2)]
    par = []
    for d, ch in jobs:
        q, k, v, col, row, _ = srcs[d]
        rows = slice(CHUNK * ch, CHUNK * (ch + 1))
        par.append(_gdn_setup(q[0, rows, :].astype(F32), k[0, rows, :].astype(F32), v[0, rows, :].astype(F32),
                              col[0, rows, :], row[0, ch, d:d + 1, :], d, masks[d]))

    span = 2
    while span < CHUNK:
        for (d, _), p in zip(jobs, par):
            p["m"] = _dot(p["m"].astype(BF16), _blockdiag(p["m"], masks[d]["hb"]))
        for (d, _), p in zip(jobs, par):
            p["p"] = p["p"] + _dot(p["p"].astype(BF16), _blockdiag(p["m"], masks[d]["hb"]))
        span *= 2
    for (d, _), p in zip(jobs, par):
        uw = _dot(_blockdiag(p["p"], masks[d]["hb"]), p["rhs"])
        p["u"], p["w"] = uw[:, 0:LANES], uw[:, LANES:2 * LANES]

    hs = lambda h: slice(CHUNK * h, CHUNK * (h + 1))
    for step in range(CHUNKS_PER_TILE):
        group = [(d, ch, p) for (d, ch), p in zip(jobs, par)][2 * step:2 * step + 2]
        chains = [(d, p, h) for d, _, p in group for h in range(nh)]
        states = [s_ref[nh * d + h] for d, _, h in chains]
        res = [_dot(jnp.concatenate([p["w"][hs(h)], p["qg"][hs(h)]], axis=0).astype(BF16), s.astype(BF16))
               for (_, p, h), s in zip(chains, states)]
        vnew = [p["u"][hs(h)] - r[0:CHUNK] for (_, p, h), r in zip(chains, res)]
        for (d, p, h), s, vn in zip(chains, states, vnew):
            g_last = p["g_last"][h]
            kg = (p["k"][hs(h)] * jnp.exp(g_last - p["g"][hs(h)])).astype(BF16)
            s_ref[nh * d + h] = s * jnp.exp(g_last) + _dot_tn(kg, vn.astype(BF16))
        for gi, (d, ch, p) in enumerate(group):
            o_intra = _dot(p["intra"], jnp.concatenate(vnew[nh * gi:nh * (gi + 1)], axis=0).astype(BF16))
            for h in range(nh):
                srcs[d][5][0, CHUNK * ch:CHUNK * (ch + 1), LANES * h:LANES * (h + 1)] = (
                    res[nh * gi + h][CHUNK:2 * CHUNK] + o_intra[hs(h)]).astype(srcs[d][5].dtype)


def _gdn(q, k, v, gcum, beta):
    nb, tt, _ = q.shape
    nt = tt // TILE
    nh = DN_HEADS
    col = jnp.concatenate([gcum, beta], axis=1).transpose(0, 2, 3, 1).reshape(nb, tt, 4 * nh)
    row = gcum.reshape(nb, 2, nh, tt // CHUNK, CHUNK).transpose(0, 3, 1, 2, 4).reshape(nb, tt // CHUNK, 2, nh * CHUNK)
    fwd = lambda b, t: (b, t, 0)
    bwd = lambda b, t: (b, jnp.where(t == 0, 0, nt - t), 0)
    tok = lambda w, im: pl.BlockSpec((1, TILE, w), im)
    rowspec = lambda im: pl.BlockSpec((1, CHUNKS_PER_TILE, 2, nh * CHUNK), lambda b, t: im(b, t) + (0,))
    side = lambda im: [tok(DN_WIDTH, im), tok(DN_WIDTH, im), tok(DN_WIDTH, im), tok(4 * nh, im), rowspec(im)]
    return pl.pallas_call(
        _gdn_kernel,
        grid=(nb, nt),
        in_specs=side(fwd) + side(bwd),
        out_specs=[tok(DN_WIDTH, fwd), tok(DN_WIDTH, bwd)],
        out_shape=[jax.ShapeDtypeStruct((nb, tt, DN_WIDTH), BF16)] * 2,
        scratch_shapes=[pltpu.VMEM((2 * nh, DN_HEAD_DIM, DN_HEAD_DIM), F32)],
        compiler_params=_params(2),
        name="gated_deltanet",
    )(q, k, v, col, row, q, k, v, col, row)


def _layernorm(x, g, b):
    mu = jnp.mean(x, axis=-1, keepdims=True)
    xc = x - mu
    var = jnp.mean(xc * xc, axis=-1, keepdims=True)
    return xc * lax.rsqrt(var + LN_EPS) * g + b


def _merge_kernel(alpha, with_router, ona_ref, of_ref, ob_ref, z_ref, x_ref, g1_ref, sh2_ref, sc2_ref,
                  nag_ref, dng_ref, wout_ref, lng_ref, lnb_ref, *rest):
    if with_router:
        router_ref, x1_ref, h2_ref, route_ref, count_ref, base_ref = rest
    else:
        x1_ref, h2_ref = rest
    ona = ona_ref[0].astype(F32)
    na = ona * lax.rsqrt(jnp.mean(ona * ona, axis=-1, keepdims=True) + RMS_EPS) * nag_ref[...]
    y = _dot(na.astype(BF16), wout_ref[0:NA_WIDTH, :])
    for h in range(DN_HEADS):
        ls = slice(LANES * h, LANES * (h + 1))
        od = of_ref[0, :, ls].astype(F32) + ob_ref[0, :, ls].astype(F32)
        dn = od * lax.rsqrt(jnp.mean(od * od, axis=-1, keepdims=True) + RMS_EPS) * dng_ref[...]
        dn = dn * _silu(z_ref[0, :, ls].astype(F32))
        y = y + _dot(dn.astype(BF16), wout_ref[NA_WIDTH + LANES * h:NA_WIDTH + LANES * (h + 1), :])
    x1 = _layernorm(alpha * x_ref[0] + g1_ref[0] * y, lng_ref[...], lnb_ref[...])
    x1_ref[0] = x1
    h2 = x1 * (1.0 + sc2_ref[0]) + sh2_ref[0]
    h2_ref[0] = h2.astype(h2_ref.dtype)
    if not with_router:
        return

    @pl.when(jnp.logical_and(pl.program_id(0) == 0, pl.program_id(1) == 0))
    def _reset():
        base_ref[...] = jnp.zeros_like(base_ref)

    h_hi = h2.astype(BF16)
    h_lo = (h2 - h_hi.astype(F32)).astype(BF16)
    logits = _dot(h_hi, router_ref[0]) + (_dot(h_lo, router_ref[0]) + _dot(h_hi, router_ref[1]))
    lane = lax.broadcasted_iota(jnp.int32, logits.shape, 1).astype(F32)
    lg = jnp.where(lane < N_EXPERTS, logits, NEG_BIG)
    m0 = jnp.max(lg, axis=-1, keepdims=True)
    e0 = jnp.min(jnp.where(lg == m0, lane, float(LANES)), axis=-1, keepdims=True)
    lg1 = jnp.where(lane == e0, NEG_BIG, lg)
    m1 = jnp.max(lg1, axis=-1, keepdims=True)
    e1 = jnp.min(jnp.where(lg1 == m1, lane, float(LANES)), axis=-1, keepdims=True)
    t1 = jnp.exp(m1 - m0)
    w0 = 1.0 / (1.0 + t1)
    w1 = t1 / (1.0 + t1)
    hot0 = lane == e0
    hot1 = lane == e1
    onehot = jnp.logical_or(hot0, hot1).astype(BF16)
    n = onehot.shape[0]
    earlier = (lax.broadcasted_iota(jnp.int32, (n, n), 0) > lax.broadcasted_iota(jnp.int32, (n, n), 1)).astype(BF16)
    before = base_ref[...] + _dot(earlier, onehot)
    rank0 = jnp.sum(jnp.where(hot0, before, 0.0), axis=-1, keepdims=True)
    rank1 = jnp.sum(jnp.where(hot1, before, 0.0), axis=-1, keepdims=True)
    total = base_ref[...] + jnp.sum(onehot.astype(F32), axis=0, keepdims=True)
    base_ref[...] = total
    count_ref[...] = jnp.broadcast_to(total, count_ref.shape)
    route = jnp.zeros_like(logits)
    for k, val in enumerate((e0, e1, w0, w1, rank0, rank1)):
        route = jnp.where(lane == k, val, route)
    route_ref[0] = route


def _merge(alpha, o_na, o_f, o_b, z, xs, mods, na_g, dn_g, w_out, ln_g, ln_b, router_p):
    nb, tt, _ = xs.shape
    nt = tt // TILE
    tok = lambda w: pl.BlockSpec((1, TILE, w), lambda b, t: (b, t, 0))
    vec = lambda w: pl.BlockSpec((1, w), lambda b, t: (0, 0))
    with_router = router_p is not None
    in_specs = [tok(NA_WIDTH), tok(DN_WIDTH), tok(DN_WIDTH), tok(DN_WIDTH), tok(D_MODEL),
                _mod_spec(2, nb), _mod_spec(3, nb), _mod_spec(4, nb), vec(NA_WIDTH), vec(DN_HEAD_DIM),
                pl.BlockSpec(w_out.shape, lambda b, t: (0, 0)), vec(D_MODEL), vec(D_MODEL)]
    args = [o_na, o_f, o_b, z, xs, mods, mods, mods, na_g.reshape(1, -1), dn_g.reshape(1, -1), w_out,
            ln_g.reshape(1, -1), ln_b.reshape(1, -1)]
    out_specs = [tok(D_MODEL), tok(D_MODEL)]
    out_shape = [jax.ShapeDtypeStruct((nb, tt, D_MODEL), F32),
                 jax.ShapeDtypeStruct((nb, tt, D_MODEL), F32 if with_router else BF16)]
    scratch = []
    if with_router:
        in_specs.append(pl.BlockSpec(router_p.shape, lambda b, t: (0, 0, 0)))
        args.append(router_p)
        out_specs += [tok(LANES), pl.BlockSpec((8, LANES), lambda b, t: (0, 0))]
        out_shape += [jax.ShapeDtypeStruct((nb, tt, LANES), F32), jax.ShapeDtypeStruct((8, LANES), F32)]
        scratch = [pltpu.VMEM((1, LANES), F32)]
    return pl.pallas_call(
        functools.partial(_merge_kernel, alpha, with_router),
        grid=(nb, nt),
        in_specs=in_specs,
        out_specs=out_specs,
        out_shape=out_shape,
        scratch_shapes=scratch,
        compiler_params=_params(2),
        name="merge_router" if with_router else "merge",
    )(*args)


def _resid_ln_kernel(alpha, x_ref, f_ref, g2_ref, lng_ref, lnb_ref, o_ref):
    o_ref[0] = _layernorm(alpha * x_ref[0] + g2_ref[0] * f_ref[0], lng_ref[...], lnb_ref[...])


def _resid_ln(alpha, xs, f, mods, ln_g, ln_b):
    nb, tt, _ = xs.shape
    tok = pl.BlockSpec((1, TILE, D_MODEL), lambda b, t: (b, t, 0))
    vec = pl.BlockSpec((1, D_MODEL), lambda b, t: (0, 0))
    return pl.pallas_call(
        functools.partial(_resid_ln_kernel, alpha),
        grid=(nb, tt // TILE),
        in_specs=[tok, tok, _mod_spec(5, nb), vec, vec],
        out_specs=tok,
        out_shape=jax.ShapeDtypeStruct(xs.shape, F32),
        compiler_params=_params(2),
        name="residual_layernorm",
    )(xs, f, mods, ln_g.reshape(1, -1), ln_b.reshape(1, -1))


def _ffn_kernel(te_ref, nact_ref, x_ref, wg_ref, wu_ref, wd_ref, o_ref, act_ref):
    del te_ref
    f = pl.program_id(1)
    active = pl.program_id(0) < nact_ref[0]

    @pl.when(jnp.logical_not(active))
    def _unused_tile():
        o_ref[...] = jnp.zeros_like(o_ref)

    @pl.when(active)
    def _active():
        x = x_ref[...].astype(BF16)
        tf = act_ref.shape[1]
        off = 0
        while off < tf:
            n = min(FFN_SUB, tf - off)
            gt = _dot(x, wg_ref[0, :, off:off + n])
            up = _dot(x, wu_ref[0, :, off:off + n])
            act_ref[:, off:off + n] = (_silu(gt) * up).astype(BF16)
            off += n
        part = _dot(act_ref[...], wd_ref[0])

        @pl.when(f == 0)
        def _first():
            o_ref[...] = part

        @pl.when(f > 0)
        def _rest():
            o_ref[...] += part


def _ffn(x, tile_expert, n_active, wg, wu, wd, tm, tf):
    s, dm = x.shape
    ff = wg.shape[2]
    nf = ff // tf
    fblk = lambda i, f, na: jnp.where(i < na[0], f, nf - 1)
    grid_spec = pltpu.PrefetchScalarGridSpec(
        num_scalar_prefetch=2,
        grid=(s // tm, nf),
        in_specs=[pl.BlockSpec((tm, dm), lambda i, f, te, na: (i, 0)),
                  pl.BlockSpec((1, dm, tf), lambda i, f, te, na: (te[i], 0, fblk(i, f, na))),
                  pl.BlockSpec((1, dm, tf), lambda i, f, te, na: (te[i], 0, fblk(i, f, na))),
                  pl.BlockSpec((1, tf, dm), lambda i, f, te, na: (te[i], fblk(i, f, na), 0))],
        out_specs=pl.BlockSpec((tm, dm), lambda i, f, te, na: (i, 0)),
        scratch_shapes=[pltpu.VMEM((tm, tf), BF16)],
    )
    return pl.pallas_call(
        _ffn_kernel,
        grid_spec=grid_spec,
        out_shape=jax.ShapeDtypeStruct((s, dm), F32),
        compiler_params=_params(2),
        name="swiglu",
    )(tile_expert, n_active, x, wg, wu, wd)


def _dense_ffn(h2, wg, wu, wd):
    nb, tt, dm = h2.shape
    s = nb * tt
    tm = FFN_TM if s % FFN_TM == 0 else TILE
    ff = wg.shape[1]
    tf = ff // 2 if (ff // 2) % LANES == 0 else ff
    n_tiles = s // tm
    out = _ffn(h2.reshape(s, dm), jnp.zeros((n_tiles,), jnp.int32), jnp.full((1,), n_tiles, jnp.int32),
               wg[None], wu[None], wd[None], tm, tf)
    return out.reshape(nb, tt, dm)


def _row_copy(src_ref, src_row, dst_ref, dst_row, sem):
    return pltpu.make_async_copy(src_ref.at[pl.ds(src_row, 1)], dst_ref.at[pl.ds(dst_row, 1)], sem)


DMA_UNROLL = 8


def _dispatch_kernel(tm, meta_ref, slot_ref, h_ref, xs_ref, zero_ref, sem):
    step = pl.program_id(0)
    n_tiles = xs_ref.shape[0] // tm

    @pl.when(step == 0)
    def _zero_unfilled():
        zero_ref[...] = jnp.zeros_like(zero_ref)

        def zero_tile(tile):
            cp = pltpu.make_async_copy(zero_ref, xs_ref.at[pl.ds(pl.multiple_of(tile * tm, tm), tm)], sem)
            cp.start()
            cp.wait()

        for e in range(N_EXPERTS):
            cnt = meta_ref[e]

            @pl.when(cnt % tm != 0)
            def _():
                zero_tile(meta_ref[N_EXPERTS + e] + cnt // tm)

        def tail(tile, carry):
            zero_tile(tile)
            return carry
        lax.fori_loop(meta_ref[2 * N_EXPERTS], n_tiles, tail, 0)

    def copies(i):
        return [_row_copy(h_ref, i, xs_ref, slot_ref[TOP_K * i + k], sem) for k in range(TOP_K)]

    def issue(i, carry):
        for cp in copies(i):
            cp.start()
        return carry

    def drain(i, carry):
        for cp in copies(i):
            cp.wait()
        return carry

    lax.fori_loop(0, TILE, issue, 0, unroll=DMA_UNROLL)
    lax.fori_loop(0, TILE, drain, 0, unroll=DMA_UNROLL)


def _dispatch(h2, slots, meta, tm, n_tiles):
    s, dm = h2.shape
    grid_spec = pltpu.PrefetchScalarGridSpec(
        num_scalar_prefetch=1,
        grid=(s // TILE,),
        in_specs=[pl.BlockSpec((TOP_K * TILE,), lambda i, meta: (i,), memory_space=pltpu.SMEM),
                  pl.BlockSpec((TILE, dm), lambda i, meta: (i, 0))],
        out_specs=pl.BlockSpec(memory_space=pl.ANY),
        scratch_shapes=[pltpu.VMEM((tm, dm), F32), pltpu.SemaphoreType.DMA(())],
    )
    return pl.pallas_call(
        functools.partial(_dispatch_kernel, tm),
        grid_spec=grid_spec,
        out_shape=jax.ShapeDtypeStruct((n_tiles * tm, dm), F32),
        compiler_params=_params(1),
        name="moe_dispatch",
    )(meta, slots, h2)


def _combine_kernel(alpha, slot_ref, next_slot_ref, x_ref, route_ref, g2_ref, lng_ref, lnb_ref, y_ref, o_ref,
                    buf_ref, sems):
    step = pl.program_id(0) * pl.num_programs(1) + pl.program_id(1)
    n_steps = pl.num_programs(0) * pl.num_programs(1)
    cur = step % 2

    def copies(slots, half, i):
        return [_row_copy(y_ref, slots[TOP_K * i + k], buf_ref.at[half, k], i, sems.at[half]) for k in range(TOP_K)]

    def issue(slots, half):
        def body(i, carry):
            for cp in copies(slots, half, i):
                cp.start()
            return carry
        lax.fori_loop(0, TILE, body, 0, unroll=DMA_UNROLL)

    @pl.when(step == 0)
    def _first():
        issue(slot_ref, cur)

    @pl.when(step + 1 < n_steps)
    def _prefetch():
        issue(next_slot_ref, 1 - cur)

    def drain(i, carry):
        for cp in copies(slot_ref, cur, i):
            cp.wait()
        return carry
    lax.fori_loop(0, TILE, drain, 0, unroll=DMA_UNROLL)

    route = route_ref[0]
    f = route[:, 2:3] * buf_ref[cur, 0] + route[:, 3:4] * buf_ref[cur, 1]
    o_ref[0] = _layernorm(alpha * x_ref[0] + g2_ref[0] * f, lng_ref[...], lnb_ref[...])


def _combine_ln(alpha, x1, route, slots, y, mods, ln_g, ln_b):
    nb, tt, dm = x1.shape
    nt = tt // TILE
    tok = lambda w: pl.BlockSpec((1, TILE, w), lambda b, t: (b, t, 0))
    vec = pl.BlockSpec((1, dm), lambda b, t: (0, 0))
    slot_block = lambda im: pl.BlockSpec((TOP_K * TILE,), im, memory_space=pltpu.SMEM)
    return pl.pallas_call(
        functools.partial(_combine_kernel, alpha),
        grid=(nb, nt),
        in_specs=[slot_block(lambda b, t: (b * nt + t,)),
                  slot_block(lambda b, t: (jnp.minimum(b * nt + t + 1, nb * nt - 1),)),
                  tok(dm), tok(LANES), _mod_spec(5, nb), vec, vec,
                  pl.BlockSpec(memory_space=pl.ANY)],
        out_specs=tok(dm),
        out_shape=jax.ShapeDtypeStruct(x1.shape, F32),
        scratch_shapes=[pltpu.VMEM((2, TOP_K, TILE, dm), F32), pltpu.SemaphoreType.DMA((2,))],
        compiler_params=_params(2),
        name="moe_combine_layernorm",
    )(slots, slots, x1, route, mods, ln_g.reshape(1, -1), ln_b.reshape(1, -1), y)


def _moe_layer(alpha, x1, h2, route, counts, mods, wg, wu, wd, ln_g, ln_b):
    nb, tt, dm = x1.shape
    s = nb * tt
    tm = FFN_TM
    ne = N_EXPERTS
    cnt = counts[0, :ne].astype(jnp.int32)
    tiles = (cnt + tm - 1) // tm
    ends = jnp.cumsum(tiles)
    first = ends - tiles
    n_active = ends[ne - 1:ne]
    n_tiles = (TOP_K * s + ne * (tm - 1)) // tm
    r = route.reshape(s, LANES)
    chosen = r[:, 0:TOP_K].astype(jnp.int32)[..., None] == jnp.arange(ne)
    slots = (jnp.sum(jnp.where(chosen, first * tm, 0), axis=-1) + r[:, 4:4 + TOP_K].astype(jnp.int32)).reshape(-1)
    i = jnp.minimum(jnp.arange(n_tiles), n_active - 1)
    te = jnp.sum(i[:, None] >= ends[None, :], axis=1).astype(jnp.int32)
    xs = _dispatch(h2.reshape(s, dm), slots, jnp.concatenate([cnt, first, n_active]), tm, n_tiles)
    y = _ffn(xs, te, n_active, wg, wu, wd, tm, wg.shape[2] // 2)
    return _combine_ln(alpha, x1, route, slots, y, mods, ln_g, ln_b)


def kernel(x, c, ctx, c_ctx, w_mod, b_mod, w_in, dn_conv_w, dn_a_log, dn_dt_bias, dn_norm_g, na_rpb, na_out_g, w_out, ln1_g, ln1_b, ln2_g, ln2_b, ffn_w_gate, ffn_w_up, ffn_w_down, moe_router, moe_w_gate, moe_w_up, moe_w_down):
    nb, seq, dm = x.shape
    depth = w_mod.shape[0]
    alpha = (2.0 * depth) ** 0.25
    n_ctx = ctx.shape[1]
    assert n_ctx == CTX_LEN == TILE and seq % TILE == 0 and dm == D_MODEL

    xs = jnp.concatenate([ctx, x], axis=1)
    rows = -(-(nb + 1) // 8) * 8
    c_all = jnp.zeros((rows, dm), F32).at[:nb].set(c).at[nb].set(c_ctx)
    mods_all = _modulation(c_all, w_mod, b_mod)
    cos, sin = _rope_tables(seq)
    p_in = w_in.shape[2]
    p_pad = 3 * NA_WIDTH + 4 * DN_WIDTH + LANES

    for l in range(depth):
        mods = mods_all[l].reshape(rows, 1, 6 * dm)
        w_in_p = jnp.zeros((dm, p_pad), BF16).at[:, :p_in].set(w_in[l].astype(BF16))
        qkv_na, qkv_dn, z, ba = _in_projection(xs, mods, w_in_p)
        o_na = _attention(qkv_na, _na_bias_table(na_rpb[l]))
        q_dn, k_dn, v_dn = _dn_prepare(qkv_dn, dn_conv_w[l], cos, sin)
        gcum, beta = _gates(ba, dn_a_log[l], dn_dt_bias[l])
        o_f, o_b = _gdn(q_dn, k_dn, v_dn, gcum, beta)
        moe = l % 2 == 1
        i = l // 2
        router_p = None
        if moe:
            r_full = jnp.zeros((dm, LANES), F32).at[:, :N_EXPERTS].set(moe_router[i])
            r_hi = r_full.astype(BF16)
            router_p = jnp.stack([r_hi, (r_full - r_hi.astype(F32)).astype(BF16)])
        merged = _merge(alpha, o_na, o_f, o_b, z, xs, mods, na_out_g[l], dn_norm_g[l], w_out[l].astype(BF16),
                        ln1_g[l], ln1_b[l], router_p)
        if moe:
            x1, h2, route, counts = merged
            xs = _moe_layer(alpha, x1, h2, route, counts, mods, moe_w_gate[i].astype(BF16),
                            moe_w_up[i].astype(BF16), moe_w_down[i].astype(BF16), ln2_g[l], ln2_b[l])
        else:
            x1, h2 = merged
            f = _dense_ffn(h2, ffn_w_gate[i].astype(BF16), ffn_w_up[i].astype(BF16), ffn_w_down[i].astype(BF16))
            xs = _resid_ln(alpha, x1, f, mods, ln2_g[l], ln2_b[l])
    return xs[:, n_ctx:]
```

```python
import functools

import jax
import jax.numpy as jnp
from jax import lax
from jax.experimental import pallas as pl
from jax.experimental.pallas import tpu as pltpu

F32 = jnp.float32
BF16 = jnp.bfloat16
HIGHEST = lax.Precision.HIGHEST

D_MODEL = 1024
GRID_W = 64
CTX_LEN = 256
NA_HEADS = 8
NA_HEAD_DIM = 64
NA_WIDTH = NA_HEADS * NA_HEAD_DIM
NA_KR = 8
NA_KC = 16
DN_HEAD_DIM = 128
DN_HEADS = 4
DN_WIDTH = DN_HEADS * DN_HEAD_DIM
DN_CONV = 5
CHUNK = 64
ROPE_THETA = 10000.0
N_EXPERTS = 8
TOP_K = 2
LN_EPS = 1e-5
RMS_EPS = 1e-6
NEG_BIG = -1e30

TILE = 256
CHUNKS_PER_TILE = TILE // CHUNK
ROWS_PER_TILE = TILE // GRID_W
LANES = 128
VMEM_LIMIT = 56 * 1024 * 1024
FFN_TM = 1024
FFN_SUB = 256


def _params(n_axes, vmem=VMEM_LIMIT):
    return pltpu.CompilerParams(dimension_semantics=("arbitrary",) * n_axes, vmem_limit_bytes=vmem)


def _dot(a, b, precision=None):
    return jnp.dot(a, b, preferred_element_type=F32, precision=precision)


def _dot_nt(a, b):
    return lax.dot_general(a, b, (((1,), (1,)), ((), ())), preferred_element_type=F32)


def _dot_tn(a, b):
    return lax.dot_general(a, b, (((0,), (0,)), ((), ())), preferred_element_type=F32)


def _silu(x):
    return x * jax.nn.sigmoid(x)


def _softplus(x):
    return jnp.maximum(x, 0.0) + jnp.log1p(jnp.exp(-jnp.abs(x)))


def _mod_spec(k, nb):
    return pl.BlockSpec((1, 1, D_MODEL), lambda b, t: (jnp.where(t == 0, nb, b), 0, k))


def _mod_kernel(c_ref, w_ref, b_ref, o_ref):
    o_ref[0] = _dot(_silu(c_ref[...]), w_ref[0], HIGHEST) + b_ref[0]


def _modulation(c_all, w_mod, b_mod):
    depth, _, n = w_mod.shape
    rows = c_all.shape[0]
    tn = 1536
    return pl.pallas_call(
        _mod_kernel,
        grid=(depth, n // tn),
        in_specs=[pl.BlockSpec((rows, D_MODEL), lambda l, j: (0, 0)),
                  pl.BlockSpec((1, D_MODEL, tn), lambda l, j: (l, 0, j)),
                  pl.BlockSpec((1, 1, tn), lambda l, j: (l, 0, j))],
        out_specs=pl.BlockSpec((1, rows, tn), lambda l, j: (l, 0, j)),
        out_shape=jax.ShapeDtypeStruct((depth, rows, n), F32),
        compiler_params=_params(2),
        name="modulation",
    )(c_all, w_mod, b_mod.reshape(depth, 1, n))


def _inproj_kernel(x_ref, sh_ref, sc_ref, w_ref, na_ref, dn_ref, z_ref, ba_ref):
    h = (x_ref[0] * (1.0 + sc_ref[0]) + sh_ref[0]).astype(BF16)
    na = _dot(h, w_ref[:, 0:3 * NA_WIDTH])
    col = lax.broadcasted_iota(jnp.int32, (1, 3 * NA_WIDTH), 1)
    na_ref[0] = (na * jnp.where(col < NA_WIDTH, NA_HEAD_DIM ** -0.5, 1.0)).astype(BF16)
    o = 3 * NA_WIDTH
    dn_ref[0] = _dot(h, w_ref[:, o:o + 3 * DN_WIDTH]).astype(dn_ref.dtype)
    o += 3 * DN_WIDTH
    z_ref[0] = _dot(h, w_ref[:, o:o + DN_WIDTH]).astype(z_ref.dtype)
    o += DN_WIDTH
    ba_ref[0] = _dot(h, w_ref[:, o:o + LANES])


def _in_projection(xs, mods, w_in_p):
    nb, tt, _ = xs.shape
    nt = tt // TILE
    tok = lambda w: pl.BlockSpec((1, TILE, w), lambda b, t: (b, t, 0))
    return pl.pallas_call(
        _inproj_kernel,
        grid=(nb, nt),
        in_specs=[tok(D_MODEL), _mod_spec(0, nb), _mod_spec(1, nb),
                  pl.BlockSpec(w_in_p.shape, lambda b, t: (0, 0))],
        out_specs=[tok(3 * NA_WIDTH), tok(3 * DN_WIDTH), tok(DN_WIDTH), tok(LANES)],
        out_shape=[jax.ShapeDtypeStruct((nb, tt, 3 * NA_WIDTH), BF16),
                   jax.ShapeDtypeStruct((nb, tt, 3 * DN_WIDTH), BF16),
                   jax.ShapeDtypeStruct((nb, tt, DN_WIDTH), BF16),
                   jax.ShapeDtypeStruct((nb, tt, LANES), F32)],
        compiler_params=_params(2),
        name="in_projection",
    )(xs, mods, mods, w_in_p)


def _na_bias_table(rpb):
    qc = jnp.arange(GRID_W)[:, None]
    kc = jnp.arange(GRID_W)[None, :]
    cstart = jnp.clip(qc - NA_KC // 2, 0, GRID_W - NA_KC)
    ok = (kc >= cstart) & (kc < cstart + NA_KC)
    left = GRID_W - NA_KC
    rp = jnp.pad(rpb, ((0, 0), (0, 0), (left, left)))
    toe = jnp.stack([rp[:, :, GRID_W - 1 - q:2 * GRID_W - 1 - q] for q in range(GRID_W)], axis=2)
    toe = jnp.where(ok, toe, NEG_BIG)
    tabs = [toe[:, NA_KR - 1 - v:2 * NA_KR - 1 - v].transpose(0, 2, 1, 3).reshape(NA_HEADS, GRID_W, NA_KR * GRID_W)
            for v in range(NA_KR)]
    return jnp.stack(tabs).reshape(NA_KR, NA_HEADS // 2, 2 * GRID_W, NA_KR * GRID_W)


def _attn_kernel(q_ref, k_ref, v_ref, bias_ref, o_ref):
    t = pl.program_id(1)
    lo = lax.broadcasted_iota(jnp.int32, (1, LANES), 1) < NA_HEAD_DIM
    hi = jnp.logical_not(lo)
    n_rows = (k_ref.shape[1] - CTX_LEN) // GRID_W
    n_keys = NA_KR * GRID_W

    def split_heads(qp):
        zero = jnp.zeros_like(qp)
        return jnp.concatenate([jnp.where(lo, qp, zero), jnp.where(hi, qp, zero)], axis=0)

    def join_heads(o, m):
        return jnp.where(lo, o[0:m], o[m:2 * m])

    @pl.when(t == 0)
    def _context():
        for pair in range(NA_HEADS // 2):
            ls = slice(LANES * pair, LANES * (pair + 1))
            s = _dot_nt(split_heads(q_ref[0, :, ls]), k_ref[0, 0:CTX_LEN, ls])
            p = jnp.exp(s - jnp.max(s, axis=-1, keepdims=True))
            o = _dot(p.astype(BF16), v_ref[0, 0:CTX_LEN, ls]) / jnp.sum(p, axis=-1, keepdims=True)
            o_ref[0, :, ls] = join_heads(o, TILE).astype(o_ref.dtype)

    @pl.when(t > 0)
    def _latent():
        for pair in range(NA_HEADS // 2):
            ls = slice(LANES * pair, LANES * (pair + 1))
            q2 = split_heads(q_ref[0, :, ls])
            s_ctx = _dot_nt(q2, k_ref[0, 0:CTX_LEN, ls])
            p_ctx, dens, o_loc = [], [], []
            for rr in range(ROWS_PER_TILE):
                r = (t - 1) * ROWS_PER_TILE + rr
                rs = jnp.clip(r - NA_KR // 2, 0, n_rows - NA_KR)
                keys = pl.ds(pl.multiple_of(CTX_LEN + rs * GRID_W, GRID_W), n_keys)
                row = lambda x: jnp.concatenate([x[GRID_W * rr:GRID_W * (rr + 1)],
                                                 x[TILE + GRID_W * rr:TILE + GRID_W * (rr + 1)]], axis=0)
                s_l = _dot_nt(row(q2), k_ref[0, keys, ls]) + bias_ref[r - rs, pair]
                s_c = row(s_ctx)
                mx = jnp.maximum(jnp.max(s_l, axis=-1, keepdims=True), jnp.max(s_c, axis=-1, keepdims=True))
                p_l = jnp.exp(s_l - mx)
                p_c = jnp.exp(s_c - mx)
                dens.append(jnp.sum(p_l, axis=-1, keepdims=True) + jnp.sum(p_c, axis=-1, keepdims=True))
                o_loc.append(_dot(p_l.astype(BF16), v_ref[0, keys, ls]))
                p_ctx.append(p_c.astype(BF16))
            o_ctx = _dot(jnp.concatenate(p_ctx, axis=0), v_ref[0, 0:CTX_LEN, ls])
            for rr in range(ROWS_PER_TILE):
                o = (o_loc[rr] + o_ctx[2 * GRID_W * rr:2 * GRID_W * (rr + 1)]) / dens[rr]
                o_ref[0, GRID_W * rr:GRID_W * (rr + 1), ls] = join_heads(o, GRID_W).astype(o_ref.dtype)


def _attention(qkv, bias_tab):
    nb, tt, _ = qkv.shape
    nt = tt // TILE
    return pl.pallas_call(
        _attn_kernel,
        grid=(nb, nt),
        in_specs=[pl.BlockSpec((1, TILE, NA_WIDTH), lambda b, t: (b, t, 0)),
                  pl.BlockSpec((1, tt, NA_WIDTH), lambda b, t: (b, 0, 1)),
                  pl.BlockSpec((1, tt, NA_WIDTH), lambda b, t: (b, 0, 2)),
                  pl.BlockSpec(bias_tab.shape, lambda b, t: (0, 0, 0, 0))],
        out_specs=pl.BlockSpec((1, TILE, NA_WIDTH), lambda b, t: (b, t, 0)),
        out_shape=jax.ShapeDtypeStruct((nb, tt, NA_WIDTH), BF16),
        compiler_params=_params(2),
        name="attention",
    )(qkv, qkv, qkv, bias_tab)


def _rope_tables(seq):
    nf = DN_HEAD_DIM // 4
    inv = ROPE_THETA ** (-jnp.arange(nf, dtype=F32) / nf)
    t = jnp.arange(seq)
    lane = jnp.arange(DN_HEAD_DIM)
    pos = jnp.where(lane[None, :] < DN_HEAD_DIM // 2, (t // GRID_W)[:, None], (t % GRID_W)[:, None]).astype(F32)
    ang = pos * inv[lane % nf][None, :]
    first = (lane % (2 * nf)) < nf
    cos = jnp.cos(ang)
    sin = jnp.where(first[None, :], -jnp.sin(ang), jnp.sin(ang))
    cos = jnp.concatenate([jnp.ones((CTX_LEN, DN_HEAD_DIM), F32), cos], 0)
    sin = jnp.concatenate([jnp.zeros((CTX_LEN, DN_HEAD_DIM), F32), sin], 0)
    return cos, sin


def _dnprep_kernel(cur_ref, prev_ref, next_ref, cw_ref, cos_ref, sin_ref, q_ref, k_ref, v_ref, pad_ref):
    t = pl.program_id(1)
    nt = pl.num_programs(1)
    halo = prev_ref.shape[1]
    prev_ok = t >= 2
    next_ok = jnp.logical_and(t >= 1, t < nt - 1)
    pad_ref[0:halo] = jnp.where(prev_ok, prev_ref[0].astype(F32), 0.0)
    pad_ref[halo:halo + TILE] = cur_ref[0].astype(F32)
    pad_ref[halo + TILE:2 * halo + TILE] = jnp.where(next_ok, next_ref[0].astype(F32), 0.0)
    lane = lax.broadcasted_iota(jnp.int32, (1, LANES), 1)
    first = (lane % (DN_HEAD_DIM // 2)) < DN_HEAD_DIM // 4
    cos = cos_ref[...]
    sin = sin_ref[...]
    outs = (q_ref, k_ref, v_ref)
    for grp in range(3 * DN_HEADS):
        ls = slice(LANES * grp, LANES * (grp + 1))
        acc = None
        for j in range(DN_CONV):
            term = cw_ref[j:j + 1, ls] * pad_ref[pl.ds(halo - DN_CONV // 2 + j, TILE), ls]
            acc = term if acc is None else acc + term
        x = _silu(acc)
        which, head = divmod(grp, DN_HEADS)
        if which < 2:
            x = x * lax.rsqrt(jnp.sum(x * x, axis=-1, keepdims=True) + RMS_EPS)
            rot = jnp.where(first, pltpu.roll(x, LANES - DN_HEAD_DIM // 4, 1), pltpu.roll(x, DN_HEAD_DIM // 4, 1))
            x = x * cos + rot * sin
        if which == 0:
            x = x * DN_HEAD_DIM ** -0.5
        outs[which][0, :, LANES * head:LANES * (head + 1)] = x.astype(outs[which].dtype)


def _dn_prepare(qkv_dn, conv_w, cos, sin):
    nb, tt, w = qkv_dn.shape
    nt = tt // TILE
    halo = 16
    hpt = TILE // halo
    cw = jnp.zeros((8, w), F32).at[:DN_CONV].set(conv_w)
    out = pl.BlockSpec((1, TILE, DN_WIDTH), lambda b, t: (b, t, 0))
    return pl.pallas_call(
        _dnprep_kernel,
        grid=(nb, nt),
        in_specs=[pl.BlockSpec((1, TILE, w), lambda b, t: (b, t, 0)),
                  pl.BlockSpec((1, halo, w), lambda b, t: (b, jnp.maximum(t * hpt - 1, 0), 0)),
                  pl.BlockSpec((1, halo, w), lambda b, t: (b, jnp.minimum((t + 1) * hpt, nt * hpt - 1), 0)),
                  pl.BlockSpec((8, w), lambda b, t: (0, 0)),
                  pl.BlockSpec((TILE, DN_HEAD_DIM), lambda b, t: (t, 0)),
                  pl.BlockSpec((TILE, DN_HEAD_DIM), lambda b, t: (t, 0))],
        out_specs=[out, out, out],
        out_shape=[jax.ShapeDtypeStruct((nb, tt, DN_WIDTH), BF16)] * 3,
        scratch_shapes=[pltpu.VMEM((TILE + 2 * halo, w), F32)],
        compiler_params=_params(2),
        name="dn_prepare",
    )(qkv_dn, qkv_dn, qkv_dn, cw, cos, sin)


def _gate_kernel(a_ref, b_ref, alog_ref, dtb_ref, g_ref, beta_ref):
    g = -jnp.exp(alog_ref[...]) * _softplus(a_ref[0] + dtb_ref[...])
    beta_ref[0] = jax.nn.sigmoid(b_ref[0])
    ii = lax.broadcasted_iota(jnp.int32, (CHUNK, CHUNK), 0)
    jj = lax.broadcasted_iota(jnp.int32, (CHUNK, CHUNK), 1)
    g_ref[0, 0] = _dot(g[0], (ii <= jj).astype(F32), HIGHEST)
    g_ref[0, 1] = _dot(g[1], (ii >= jj).astype(F32), HIGHEST)


def _gates(ba, a_log, dt_bias):
    nb, tt, _ = ba.shape
    nc = tt // CHUNK
    nh = DN_HEADS
    rows = lambda x: x.transpose(0, 2, 1).reshape(nb, 2, nh * nc, CHUNK)
    spec = pl.BlockSpec((1, 2, nh * nc, CHUNK), lambda b: (b, 0, 0, 0))
    par = lambda p: jnp.broadcast_to(p.reshape(2, nh, 1, 1).astype(F32), (2, nh, nc, CHUNK)).reshape(2, nh * nc, CHUNK)
    pspec = pl.BlockSpec((2, nh * nc, CHUNK), lambda b: (0, 0, 0))
    gcum, beta = pl.pallas_call(
        _gate_kernel,
        grid=(nb,),
        in_specs=[spec, spec, pspec, pspec],
        out_specs=[spec, spec],
        out_shape=[jax.ShapeDtypeStruct((nb, 2, nh * nc, CHUNK), F32)] * 2,
        compiler_params=_params(1),
        name="dn_gates",
    )(rows(ba[..., 2 * nh:4 * nh]), rows(ba[..., 0:2 * nh]), par(a_log), par(dt_bias))
    return gcum.reshape(nb, 2 * nh, nc, CHUNK), beta.reshape(nb, 2 * nh, nc, CHUNK)


def _gdn_masks(d):
    width = DN_HEADS * CHUNK
    lane = lax.broadcasted_iota(jnp.int32, (CHUNK, width), 1)
    ri = lax.broadcasted_iota(jnp.int32, (CHUNK, width), 0)
    jj = lane % CHUNK
    lower = (ri >= jj) if d == 0 else (ri <= jj)
    strict = (ri > jj) if d == 0 else (ri < jj)
    return dict(hb=lane // CHUNK, lower=lower, strict=strict, eye=(ri == jj).astype(F32))


def _blockdiag(xp, hb):
    return jnp.concatenate([jnp.where(hb == h, xp, 0.0) for h in range(DN_HEADS)], axis=0).astype(BF16)


def _gdn_setup(q, k, v, col, row, d, mk):
    nh = DN_HEADS
    hb = mk["hb"]
    stack = lambda x: jnp.concatenate([x[:, LANES * h:LANES * (h + 1)] for h in range(nh)], axis=0)
    colb = lambda c, w: jnp.broadcast_to(col[:, c:c + 1], (CHUNK, w))
    ks, qs, vs = stack(k), stack(q), stack(v)
    gs = jnp.concatenate([colb(nh * d + h, LANES) for h in range(nh)], axis=0)
    bs = jnp.concatenate([colb(2 * nh + nh * d + h, LANES) for h in range(nh)], axis=0)
    eg = jnp.exp(gs)
    kbs = ks * bs
    rhs = jnp.concatenate([vs * bs, kbs * eg], axis=1).astype(BF16)
    full = _dot_nt(jnp.concatenate([kbs, qs], axis=0).astype(BF16), ks.astype(BF16))

    def pack(lo):
        acc = full[lo:lo + CHUNK]
        for h in range(1, nh):
            acc = jnp.where(hb == h, full[lo + CHUNK * h:lo + CHUNK * (h + 1)], acc)
        return acc

    cp = colb(nh * d, nh * CHUNK)
    for h in range(1, nh):
        cp = jnp.where(hb == h, colb(nh * d + h, nh * CHUNK), cp)
    decay = jnp.exp(jnp.where(mk["lower"], cp - row, NEG_BIG))
    m_p = -jnp.where(mk["strict"], pack(0) * decay, 0.0)
    last = CHUNK - 1 if d == 0 else 0
    g_last = [gs[CHUNK * h + last:CHUNK * h + last + 1] for h in range(nh)]
    return dict(m=m_p, p=mk["eye"] + m_p, rhs=rhs, qg=qs * eg, k=ks, g=gs, g_last=g_last,
                intra=_blockdiag(pack(nh * CHUNK) * decay, hb))


def _gdn_kernel(qf, kf, vf, colf, rowf, qb, kb, vb, colb, rowb, of_ref, ob_ref, s_ref):
    @pl.when(pl.program_id(1) == 0)
    def _reset():
        s_ref[...] = jnp.zeros_like(s_ref)

    nh = DN_HEADS
    srcs = ((qf, kf, vf, colf, rowf, of_ref), (qb, kb, vb, colb, rowb, ob_ref))
    masks = (_gdn_masks(0), _gdn_masks(1))
    jobs = [(d, step if d == 0 else CHUNKS_PER_TILE - 1 - step)
            for step in range(CHUNKS_PER_TILE) for d in range(2)]
    par = []
    for d, ch in jobs:
        q, k, v, col, row, _ = srcs[d]
        rows = slice(CHUNK * ch, CHUNK * (ch + 1))
        par.append(_gdn_setup(q[0, rows, :].astype(F32), k[0, rows, :].astype(F32), v[0, rows, :].astype(F32),
                              col[0, rows, :], row[0, ch, d:d + 1, :], d, masks[d]))

    span = 2
    while span < CHUNK:
        for (d, _), p in zip(jobs, par):
            p["m"] = _dot(p["m"].astype(BF16), _blockdiag(p["m"], masks[d]["hb"]))
        for (d, _), p in zip(jobs, par):
            p["p"] = p["p"] + _dot(p["p"].astype(BF16), _blockdiag(p["m"], masks[d]["hb"]))
        span *= 2
    for (d, _), p in zip(jobs, par):
        uw = _dot(_blockdiag(p["p"], masks[d]["hb"]), p["rhs"])
        p["u"], p["w"] = uw[:, 0:LANES], uw[:, LANES:2 * LANES]

    hs = lambda h: slice(CHUNK * h, CHUNK * (h + 1))
    for step in range(CHUNKS_PER_TILE):
        group = [(d, ch, p) for (d, ch), p in zip(jobs, par)][2 * step:2 * step + 2]
        chains = [(d, p, h) for d, _, p in group for h in range(nh)]
        states = [s_ref[nh * d + h] for d, _, h in chains]
        res = [_dot(jnp.concatenate([p["w"][hs(h)], p["qg"][hs(h)]], axis=0).astype(BF16), s.astype(BF16))
               for (_, p, h), s in zip(chains, states)]
        vnew = [p["u"][hs(h)] - r[0:CHUNK] for (_, p, h), r in zip(chains, res)]
        for (d, p, h), s, vn in zip(chains, states, vnew):
            g_last = p["g_last"][h]
            kg = (p["k"][hs(h)] * jnp.exp(g_last - p["g"][hs(h)])).astype(BF16)
            s_ref[nh * d + h] = s * jnp.exp(g_last) + _dot_tn(kg, vn.astype(BF16))
        for gi, (d, ch, p) in enumerate(group):
            o_intra = _dot(p["intra"], jnp.concatenate(vnew[nh * gi:nh * (gi + 1)], axis=0).astype(BF16))
            for h in range(nh):
                srcs[d][5][0, CHUNK * ch:CHUNK * (ch + 1), LANES * h:LANES * (h + 1)] = (
                    res[nh * gi + h][CHUNK:2 * CHUNK] + o_intra[hs(h)]).astype(srcs[d][5].dtype)


def _gdn(q, k, v, gcum, beta):
    nb, tt, _ = q.shape
    nt = tt // TILE
    nh = DN_HEADS
    col = jnp.concatenate([gcum, beta], axis=1).transpose(0, 2, 3, 1).reshape(nb, tt, 4 * nh)
    row = gcum.reshape(nb, 2, nh, tt // CHUNK, CHUNK).transpose(0, 3, 1, 2, 4).reshape(nb, tt // CHUNK, 2, nh * CHUNK)
    fwd = lambda b, t: (b, t, 0)
    bwd = lambda b, t: (b, jnp.where(t == 0, 0, nt - t), 0)
    tok = lambda w, im: pl.BlockSpec((1, TILE, w), im)
    rowspec = lambda im: pl.BlockSpec((1, CHUNKS_PER_TILE, 2, nh * CHUNK), lambda b, t: im(b, t) + (0,))
    side = lambda im: [tok(DN_WIDTH, im), tok(DN_WIDTH, im), tok(DN_WIDTH, im), tok(4 * nh, im), rowspec(im)]
    return pl.pallas_call(
        _gdn_kernel,
        grid=(nb, nt),
        in_specs=side(fwd) + side(bwd),
        out_specs=[tok(DN_WIDTH, fwd), tok(DN_WIDTH, bwd)],
        out_shape=[jax.ShapeDtypeStruct((nb, tt, DN_WIDTH), BF16)] * 2,
        scratch_shapes=[pltpu.VMEM((2 * nh, DN_HEAD_DIM, DN_HEAD_DIM), F32)],
        compiler_params=_params(2),
        name="gated_deltanet",
    )(q, k, v, col, row, q, k, v, col, row)


def _layernorm(x, g, b):
    mu = jnp.mean(x, axis=-1, keepdims=True)
    xc = x - mu
    var = jnp.mean(xc * xc, axis=-1, keepdims=True)
    return xc * lax.rsqrt(var + LN_EPS) * g + b


def _merge_kernel(alpha, with_router, ona_ref, of_ref, ob_ref, z_ref, x_ref, g1_ref, sh2_ref, sc2_ref,
                  nag_ref, dng_ref, wout_ref, lng_ref, lnb_ref, *rest):
    if with_router:
        router_ref, x1_ref, h2_ref, route_ref, count_ref, base_ref = rest
    else:
        x1_ref, h2_ref = rest
    ona = ona_ref[0].astype(F32)
    na = ona * lax.rsqrt(jnp.mean(ona * ona, axis=-1, keepdims=True) + RMS_EPS) * nag_ref[...]
    y = _dot(na.astype(BF16), wout_ref[0:NA_WIDTH, :])
    for h in range(DN_HEADS):
        ls = slice(LANES * h, LANES * (h + 1))
        od = of_ref[0, :, ls].astype(F32) + ob_ref[0, :, ls].astype(F32)
        dn = od * lax.rsqrt(jnp.mean(od * od, axis=-1, keepdims=True) + RMS_EPS) * dng_ref[...]
        dn = dn * _silu(z_ref[0, :, ls].astype(F32))
        y = y + _dot(dn.astype(BF16), wout_ref[NA_WIDTH + LANES * h:NA_WIDTH + LANES * (h + 1), :])
    x1 = _layernorm(alpha * x_ref[0] + g1_ref[0] * y, lng_ref[...], lnb_ref[...])
    x1_ref[0] = x1
    h2 = x1 * (1.0 + sc2_ref[0]) + sh2_ref[0]
    h2_ref[0] = h2.astype(h2_ref.dtype)
    if not with_router:
        return

    @pl.when(jnp.logical_and(pl.program_id(0) == 0, pl.program_id(1) == 0))
    def _reset():
        base_ref[...] = jnp.zeros_like(base_ref)

    h_hi = h2.astype(BF16)
    h_lo = (h2 - h_hi.astype(F32)).astype(BF16)
    logits = _dot(h_hi, router_ref[0]) + (_dot(h_lo, router_ref[0]) + _dot(h_hi, router_ref[1]))
    lane = lax.broadcasted_iota(jnp.int32, logits.shape, 1).astype(F32)
    lg = jnp.where(lane < N_EXPERTS, logits, NEG_BIG)
    m0 = jnp.max(lg, axis=-1, keepdims=True)
    e0 = jnp.min(jnp.where(lg == m0, lane, float(LANES)), axis=-1, keepdims=True)
    lg1 = jnp.where(lane == e0, NEG_BIG, lg)
    m1 = jnp.max(lg1, axis=-1, keepdims=True)
    e1 = jnp.min(jnp.where(lg1 == m1, lane, float(LANES)), axis=-1, keepdims=True)
    t1 = jnp.exp(m1 - m0)
    w0 = 1.0 / (1.0 + t1)
    w1 = t1 / (1.0 + t1)
    hot0 = lane == e0
    hot1 = lane == e1
    onehot = jnp.logical_or(hot0, hot1).astype(BF16)
    n = onehot.shape[0]
    earlier = (lax.broadcasted_iota(jnp.int32, (n, n), 0) > lax.broadcasted_iota(jnp.int32, (n, n), 1)).astype(BF16)
    before = base_ref[...] + _dot(earlier, onehot)
    rank0 = jnp.sum(jnp.where(hot0, before, 0.0), axis=-1, keepdims=True)
    rank1 = jnp.sum(jnp.where(hot1, before, 0.0), axis=-1, keepdims=True)
    total = base_ref[...] + jnp.sum(onehot.astype(F32), axis=0, keepdims=True)
    base_ref[...] = total
    count_ref[...] = jnp.broadcast_to(total, count_ref.shape)
    route = jnp.zeros_like(logits)
    for k, val in enumerate((e0, e1, w0, w1, rank0, rank1)):
        route = jnp.where(lane == k, val, route)
    route_ref[0] = route


def _merge(alpha, o_na, o_f, o_b, z, xs, mods, na_g, dn_g, w_out, ln_g, ln_b, router_p):
    nb, tt, _ = xs.shape
    nt = tt // TILE
    tok = lambda w: pl.BlockSpec((1, TILE, w), lambda b, t: (b, t, 0))
    vec = lambda w: pl.BlockSpec((1, w), lambda b, t: (0, 0))
    with_router = router_p is not None
    in_specs = [tok(NA_WIDTH), tok(DN_WIDTH), tok(DN_WIDTH), tok(DN_WIDTH), tok(D_MODEL),
                _mod_spec(2, nb), _mod_spec(3, nb), _mod_spec(4, nb), vec(NA_WIDTH), vec(DN_HEAD_DIM),
                pl.BlockSpec(w_out.shape, lambda b, t: (0, 0)), vec(D_MODEL), vec(D_MODEL)]
    args = [o_na, o_f, o_b, z, xs, mods, mods, mods, na_g.reshape(1, -1), dn_g.reshape(1, -1), w_out,
            ln_g.reshape(1, -1), ln_b.reshape(1, -1)]
    out_specs = [tok(D_MODEL), tok(D_MODEL)]
    out_shape = [jax.ShapeDtypeStruct((nb, tt, D_MODEL), F32),
                 jax.ShapeDtypeStruct((nb, tt, D_MODEL), F32 if with_router else BF16)]
    scratch = []
    if with_router:
        in_specs.append(pl.BlockSpec(router_p.shape, lambda b, t: (0, 0, 0)))
        args.append(router_p)
        out_specs += [tok(LANES), pl.BlockSpec((8, LANES), lambda b, t: (0, 0))]
        out_shape += [jax.ShapeDtypeStruct((nb, tt, LANES), F32), jax.ShapeDtypeStruct((8, LANES), F32)]
        scratch = [pltpu.VMEM((1, LANES), F32)]
    return pl.pallas_call(
        functools.partial(_merge_kernel, alpha, with_router),
        grid=(nb, nt),
        in_specs=in_specs,
        out_specs=out_specs,
        out_shape=out_shape,
        scratch_shapes=scratch,
        compiler_params=_params(2),
        name="merge_router" if with_router else "merge",
    )(*args)


def _resid_ln_kernel(alpha, x_ref, f_ref, g2_ref, lng_ref, lnb_ref, o_ref):
    o_ref[0] = _layernorm(alpha * x_ref[0] + g2_ref[0] * f_ref[0], lng_ref[...], lnb_ref[...])


def _resid_ln(alpha, xs, f, mods, ln_g, ln_b):
    nb, tt, _ = xs.shape
    tok = pl.BlockSpec((1, TILE, D_MODEL), lambda b, t: (b, t, 0))
    vec = pl.BlockSpec((1, D_MODEL), lambda b, t: (0, 0))
    return pl.pallas_call(
        functools.partial(_resid_ln_kernel, alpha),
        grid=(nb, tt // TILE),
        in_specs=[tok, tok, _mod_spec(5, nb), vec, vec],
        out_specs=tok,
        out_shape=jax.ShapeDtypeStruct(xs.shape, F32),
        compiler_params=_params(2),
        name="residual_layernorm",
    )(xs, f, mods, ln_g.reshape(1, -1), ln_b.reshape(1, -1))


def _ffn_kernel(te_ref, nact_ref, x_ref, wg_ref, wu_ref, wd_ref, o_ref, act_ref):
    del te_ref
    f = pl.program_id(1)
    active = pl.program_id(0) < nact_ref[0]

    @pl.when(jnp.logical_not(active))
    def _unused_tile():
        o_ref[...] = jnp.zeros_like(o_ref)

    @pl.when(active)
    def _active():
        x = x_ref[...].astype(BF16)
        tf = act_ref.shape[1]
        off = 0
        while off < tf:
            n = min(FFN_SUB, tf - off)
            gt = _dot(x, wg_ref[0, :, off:off + n])
            up = _dot(x, wu_ref[0, :, off:off + n])
            act_ref[:, off:off + n] = (_silu(gt) * up).astype(BF16)
            off += n
        part = _dot(act_ref[...], wd_ref[0])

        @pl.when(f == 0)
        def _first():
            o_ref[...] = part

        @pl.when(f > 0)
        def _rest():
            o_ref[...] += part


def _ffn(x, tile_expert, n_active, wg, wu, wd, tm, tf):
    s, dm = x.shape
    ff = wg.shape[2]
    nf = ff // tf
    fblk = lambda i, f, na: jnp.where(i < na[0], f, nf - 1)
    grid_spec = pltpu.PrefetchScalarGridSpec(
        num_scalar_prefetch=2,
        grid=(s // tm, nf),
        in_specs=[pl.BlockSpec((tm, dm), lambda i, f, te, na: (i, 0)),
                  pl.BlockSpec((1, dm, tf), lambda i, f, te, na: (te[i], 0, fblk(i, f, na))),
                  pl.BlockSpec((1, dm, tf), lambda i, f, te, na: (te[i], 0, fblk(i, f, na))),
                  pl.BlockSpec((1, tf, dm), lambda i, f, te, na: (te[i], fblk(i, f, na), 0))],
        out_specs=pl.BlockSpec((tm, dm), lambda i, f, te, na: (i, 0)),
        scratch_shapes=[pltpu.VMEM((tm, tf), BF16)],
    )
    return pl.pallas_call(
        _ffn_kernel,
        grid_spec=grid_spec,
        out_shape=jax.ShapeDtypeStruct((s, dm), F32),
        compiler_params=_params(2),
        name="swiglu",
    )(tile_expert, n_active, x, wg, wu, wd)


def _dense_ffn(h2, wg, wu, wd):
    nb, tt, dm = h2.shape
    s = nb * tt
    tm = FFN_TM if s % FFN_TM == 0 else TILE
    ff = wg.shape[1]
    tf = ff // 2 if (ff // 2) % LANES == 0 else ff
    n_tiles = s // tm
    out = _ffn(h2.reshape(s, dm), jnp.zeros((n_tiles,), jnp.int32), jnp.full((1,), n_tiles, jnp.int32),
               wg[None], wu[None], wd[None], tm, tf)
    return out.reshape(nb, tt, dm)


def _row_copy(src_ref, src_row, dst_ref, dst_row, sem):
    return pltpu.make_async_copy(src_ref.at[pl.ds(src_row, 1)], dst_ref.at[pl.ds(dst_row, 1)], sem)


DMA_UNROLL = 16


def _dispatch_kernel(tm, meta_ref, slot_ref, h_ref, xs_ref, zero_ref, sem):
    step = pl.program_id(0)
    n_tiles = xs_ref.shape[0] // tm

    @pl.when(step == 0)
    def _zero_unfilled():
        zero_ref[...] = jnp.zeros_like(zero_ref)

        def zero_tile(tile):
            cp = pltpu.make_async_copy(zero_ref, xs_ref.at[pl.ds(pl.multiple_of(tile * tm, tm), tm)], sem)
            cp.start()
            cp.wait()

        for e in range(N_EXPERTS):
            cnt = meta_ref[e]

            @pl.when(cnt % tm != 0)
            def _():
                zero_tile(meta_ref[N_EXPERTS + e] + cnt // tm)

        def tail(tile, carry):
            zero_tile(tile)
            return carry
        lax.fori_loop(meta_ref[2 * N_EXPERTS], n_tiles, tail, 0)

    def copies(i):
        return [_row_copy(h_ref, i, xs_ref, slot_ref[TOP_K * i + k], sem) for k in range(TOP_K)]

    def issue(i, carry):
        for k, cp in enumerate(copies(i)):
            cp.start(priority=k % 2)
        return carry

    def drain(i, carry):
        for cp in copies(i):
            cp.wait()
        return carry

    lax.fori_loop(0, TILE, issue, 0, unroll=DMA_UNROLL)
    lax.fori_loop(0, TILE, drain, 0, unroll=DMA_UNROLL)


def _dispatch(h2, slots, meta, tm, n_tiles):
    s, dm = h2.shape
    grid_spec = pltpu.PrefetchScalarGridSpec(
        num_scalar_prefetch=1,
        grid=(s // TILE,),
        in_specs=[pl.BlockSpec((TOP_K * TILE,), lambda i, meta: (i,), memory_space=pltpu.SMEM),
                  pl.BlockSpec((TILE, dm), lambda i, meta: (i, 0))],
        out_specs=pl.BlockSpec(memory_space=pl.ANY),
        scratch_shapes=[pltpu.VMEM((tm, dm), F32), pltpu.SemaphoreType.DMA(())],
    )
    return pl.pallas_call(
        functools.partial(_dispatch_kernel, tm),
        grid_spec=grid_spec,
        out_shape=jax.ShapeDtypeStruct((n_tiles * tm, dm), F32),
        compiler_params=_params(1),
        name="moe_dispatch",
    )(meta, slots, h2)


def _combine_kernel(alpha, slot_ref, next_slot_ref, x_ref, route_ref, g2_ref, lng_ref, lnb_ref, y_ref, o_ref,
                    buf_ref, sems):
    step = pl.program_id(0) * pl.num_programs(1) + pl.program_id(1)
    n_steps = pl.num_programs(0) * pl.num_programs(1)
    cur = step % 2

    def copies(slots, half, i):
        return [_row_copy(y_ref, slots[TOP_K * i + k], buf_ref.at[half, k], i, sems.at[half]) for k in range(TOP_K)]

    def issue(slots, half):
        def body(i, carry):
            for k, cp in enumerate(copies(slots, half, i)):
                cp.start(priority=k % 2)
            return carry
        lax.fori_loop(0, TILE, body, 0, unroll=DMA_UNROLL)

    @pl.when(step == 0)
    def _first():
        issue(slot_ref, cur)

    @pl.when(step + 1 < n_steps)
    def _prefetch():
        issue(next_slot_ref, 1 - cur)

    def drain(i, carry):
        for cp in copies(slot_ref, cur, i):
            cp.wait()
        return carry
    lax.fori_loop(0, TILE, drain, 0, unroll=DMA_UNROLL)

    route = route_ref[0]
    f = route[:, 2:3] * buf_ref[cur, 0] + route[:, 3:4] * buf_ref[cur, 1]
    o_ref[0] = _layernorm(alpha * x_ref[0] + g2_ref[0] * f, lng_ref[...], lnb_ref[...])


def _combine_ln(alpha, x1, route, slots, y, mods, ln_g, ln_b):
    nb, tt, dm = x1.shape
    nt = tt // TILE
    tok = lambda w: pl.BlockSpec((1, TILE, w), lambda b, t: (b, t, 0))
    vec = pl.BlockSpec((1, dm), lambda b, t: (0, 0))
    slot_block = lambda im: pl.BlockSpec((TOP_K * TILE,), im, memory_space=pltpu.SMEM)
    return pl.pallas_call(
        functools.partial(_combine_kernel, alpha),
        grid=(nb, nt),
        in_specs=[slot_block(lambda b, t: (b * nt + t,)),
                  slot_block(lambda b, t: (jnp.minimum(b * nt + t + 1, nb * nt - 1),)),
                  tok(dm), tok(LANES), _mod_spec(5, nb), vec, vec,
                  pl.BlockSpec(memory_space=pl.ANY)],
        out_specs=tok(dm),
        out_shape=jax.ShapeDtypeStruct(x1.shape, F32),
        scratch_shapes=[pltpu.VMEM((2, TOP_K, TILE, dm), F32), pltpu.SemaphoreType.DMA((2,))],
        compiler_params=_params(2),
        name="moe_combine_layernorm",
    )(slots, slots, x1, route, mods, ln_g.reshape(1, -1), ln_b.reshape(1, -1), y)


def _moe_layer(alpha, x1, h2, route, counts, mods, wg, wu, wd, ln_g, ln_b):
    nb, tt, dm = x1.shape
    s = nb * tt
    tm = FFN_TM
    ne = N_EXPERTS
    cnt = counts[0, :ne].astype(jnp.int32)
    tiles = (cnt + tm - 1) // tm
    ends = jnp.cumsum(tiles)
    first = ends - tiles
    n_active = ends[ne - 1:ne]
    n_tiles = (TOP_K * s + ne * (tm - 1)) // tm
    r = route.reshape(s, LANES)
    chosen = r[:, 0:TOP_K].astype(jnp.int32)[..., None] == jnp.arange(ne)
    slots = (jnp.sum(jnp.where(chosen, first * tm, 0), axis=-1) + r[:, 4:4 + TOP_K].astype(jnp.int32)).reshape(-1)
    i = jnp.minimum(jnp.arange(n_tiles), n_active - 1)
    te = jnp.sum(i[:, None] >= ends[None, :], axis=1).astype(jnp.int32)
    xs = _dispatch(h2.reshape(s, dm), slots, jnp.concatenate([cnt, first, n_active]), tm, n_tiles)
    y = _ffn(xs, te, n_active, wg, wu, wd, tm, wg.shape[2] // 2)
    return _combine_ln(alpha, x1, route, slots, y, mods, ln_g, ln_b)


def kernel(x, c, ctx, c_ctx, w_mod, b_mod, w_in, dn_conv_w, dn_a_log, dn_dt_bias, dn_norm_g, na_rpb, na_out_g, w_out, ln1_g, ln1_b, ln2_g, ln2_b, ffn_w_gate, ffn_w_up, ffn_w_down, moe_router, moe_w_gate, moe_w_up, moe_w_down):
    nb, seq, dm = x.shape
    depth = w_mod.shape[0]
    alpha = (2.0 * depth) ** 0.25
    n_ctx = ctx.shape[1]
    assert n_ctx == CTX_LEN == TILE and seq % TILE == 0 and dm == D_MODEL

    xs = jnp.concatenate([ctx, x], axis=1)
    rows = -(-(nb + 1) // 8) * 8
    c_all = jnp.zeros((rows, dm), F32).at[:nb].set(c).at[nb].set(c_ctx)
    mods_all = _modulation(c_all, w_mod, b_mod)
    cos, sin = _rope_tables(seq)
    p_in = w_in.shape[2]
    p_pad = 3 * NA_WIDTH + 4 * DN_WIDTH + LANES

    for l in range(depth):
        mods = mods_all[l].reshape(rows, 1, 6 * dm)
        w_in_p = jnp.zeros((dm, p_pad), BF16).at[:, :p_in].set(w_in[l].astype(BF16))
        qkv_na, qkv_dn, z, ba = _in_projection(xs, mods, w_in_p)
        o_na = _attention(qkv_na, _na_bias_table(na_rpb[l]))
        q_dn, k_dn, v_dn = _dn_prepare(qkv_dn, dn_conv_w[l], cos, sin)
        gcum, beta = _gates(ba, dn_a_log[l], dn_dt_bias[l])
        o_f, o_b = _gdn(q_dn, k_dn, v_dn, gcum, beta)
        moe = l % 2 == 1
        i = l // 2
        router_p = None
        if moe:
            r_full = jnp.zeros((dm, LANES), F32).at[:, :N_EXPERTS].set(moe_router[i])
            r_hi = r_full.astype(BF16)
            router_p = jnp.stack([r_hi, (r_full - r_hi.astype(F32)).astype(BF16)])
        merged = _merge(alpha, o_na, o_f, o_b, z, xs, mods, na_out_g[l], dn_norm_g[l], w_out[l].astype(BF16),
                        ln1_g[l], ln1_b[l], router_p)
        if moe:
            x1, h2, route, counts = merged
            xs = _moe_layer(alpha, x1, h2, route, counts, mods, moe_w_gate[i].astype(BF16),
                            moe_w_up[i].astype(BF16), moe_w_down[i].astype(BF16), ln2_g[l], ln2_b[l])
        else:
            x1, h2 = merged
            f = _dense_ffn(h2, ffn_w_gate[i].astype(BF16), ffn_w_up[i].astype(BF16), ffn_w_down[i].astype(BF16))
            xs = _resid_ln(alpha, x1, f, mods, ln2_g[l], ln2_b[l])
    return xs[:, n_ctx:]
```
